```python
import jax, jax.numpy as jnp
from jax import lax
import numpy as np

D_MODEL = 1024
BATCH = 8
SEQ = 4096
DEPTH = 4

N_A_LAYERS = DEPTH // 2
N_B_LAYERS = DEPTH - N_A_LAYERS
SSM_EXPAND = 2
D_INNER = SSM_EXPAND * D_MODEL
SSM_HEAD_DIM = 64
SSM_HEADS = D_INNER // SSM_HEAD_DIM
SSM_GROUPS = 4
SSM_HPG = SSM_HEADS // SSM_GROUPS
D_STATE = 128
CONV_WIDTH = 4
CONV_CH = D_INNER + 2 * SSM_GROUPS * D_STATE
SSM_IN = D_INNER + CONV_CH + SSM_HEADS
CHUNK = 128
ATT_HEADS = 16
KV_HEADS = 4
HEAD_DIM = 64
Q_PER_KV = ATT_HEADS // KV_HEADS
ATT_WIDTH = ATT_HEADS * HEAD_DIM
KV_WIDTH = KV_HEADS * HEAD_DIM
WINDOW = 128
BLOCK = WINDOW
ROPE_THETA = 10000.0
PLE_DIM = 256
EPS = 1e-6

kernel_name = 'yoco_ssd_swa_sink_hybrid'


def rms_norm(x, w):
    xf = x.astype(jnp.float32)
    y = xf * lax.rsqrt(jnp.mean(xf * xf, axis=-1, keepdims=True) + EPS)
    return (y * w.astype(jnp.float32)).astype(x.dtype)


def rope(x, positions):
    half = HEAD_DIM // 2
    inv_freq = ROPE_THETA ** (-(jnp.arange(half, dtype=jnp.float32) * 2.0 / HEAD_DIM))
    ang = positions.astype(jnp.float32)[..., None] * inv_freq
    cos = jnp.cos(ang)[:, :, None, :]
    sin = jnp.sin(ang)[:, :, None, :]
    xf = x.astype(jnp.float32)
    x1, x2 = xf[..., :half], xf[..., half:]
    return jnp.concatenate([x1 * cos - x2 * sin, x2 * cos + x1 * sin], axis=-1).astype(x.dtype)


def ssd_chunked(xs, dt, a, bm, cm):
    b, s = xs.shape[:2]
    nc = s // CHUNK
    x = xs.astype(jnp.float32).reshape(b, nc, CHUNK, SSM_GROUPS, SSM_HPG, SSM_HEAD_DIM)
    dt = dt.reshape(b, nc, CHUNK, SSM_GROUPS, SSM_HPG)
    bm = bm.astype(jnp.float32).reshape(b, nc, CHUNK, SSM_GROUPS, D_STATE)
    cm = cm.astype(jnp.float32).reshape(b, nc, CHUNK, SSM_GROUPS, D_STATE)
    a_cum = jnp.cumsum(dt * a.reshape(SSM_GROUPS, SSM_HPG), axis=2)
    xdt = x * dt[..., None]
    seg = a_cum[:, :, :, None] - a_cum[:, :, None, :]
    causal = jnp.tril(jnp.ones((CHUNK, CHUNK), bool))[None, None, :, :, None, None]
    decay = jnp.exp(jnp.where(causal, seg, -jnp.inf))
    cb = jnp.einsum('bcign,bcjgn->bcijg', cm, bm)
    y_diag = jnp.einsum('bcijg,bcijgr,bcjgrp->bcigrp', cb, decay, xdt)
    decay_to_end = jnp.exp(a_cum[:, :, -1:] - a_cum)
    states = jnp.einsum('bcjgn,bcjgr,bcjgrp->bcgrpn', bm, decay_to_end, xdt)
    chunk_decay = jnp.exp(a_cum[:, :, -1])

    def step(carry, inp):
        st, dec = inp
        return carry * dec[..., None, None] + st, carry

    init = jnp.zeros((b, SSM_GROUPS, SSM_HPG, SSM_HEAD_DIM, D_STATE), jnp.float32)
    _, prev = lax.scan(step, init, (jnp.moveaxis(states, 1, 0), jnp.moveaxis(chunk_decay, 1, 0)))
    prev = jnp.moveaxis(prev, 0, 1)
    y_off = jnp.einsum('bcign,bcgrpn,bcigr->bcigrp', cm, prev, jnp.exp(a_cum))
    return (y_diag + y_off).reshape(b, s, SSM_HEADS, SSM_HEAD_DIM)


def mamba2_mixer(h, norm_w, in_w, conv_w, conv_b, dt_bias, a_log, d_skip, gnorm_w, out_w):
    b, s, _ = h.shape
    u = rms_norm(h, norm_w)
    zxbcdt = u @ in_w
    z, xbc, dt = jnp.split(zxbcdt, [D_INNER, D_INNER + CONV_CH], axis=-1)
    xbc = lax.conv_general_dilated(xbc, conv_w[:, None, :], window_strides=(1,),
                                   padding=[(CONV_WIDTH - 1, 0)],
                                   dimension_numbers=('NWC', 'WIO', 'NWC'),
                                   feature_group_count=CONV_CH) + conv_b
    xbc = jax.nn.silu(xbc)
    xs, bm, cm = jnp.split(xbc, [D_INNER, D_INNER + SSM_GROUPS * D_STATE], axis=-1)
    xs = xs.reshape(b, s, SSM_HEADS, SSM_HEAD_DIM)
    bm = bm.reshape(b, s, SSM_GROUPS, D_STATE)
    cm = cm.reshape(b, s, SSM_GROUPS, D_STATE)
    dt = jax.nn.softplus(dt.astype(jnp.float32) + dt_bias.astype(jnp.float32))
    a = -jnp.exp(a_log.astype(jnp.float32))
    y = ssd_chunked(xs, dt, a, bm, cm)
    y = y + d_skip.astype(jnp.float32)[:, None] * xs.astype(jnp.float32)
    y = y.reshape(b, s, D_INNER) * jax.nn.silu(z.astype(jnp.float32))
    yg = y.reshape(b, s, SSM_GROUPS, D_INNER // SSM_GROUPS)
    yg = yg * lax.rsqrt(jnp.mean(yg * yg, axis=-1, keepdims=True) + EPS)
    y = yg.reshape(b, s, D_INNER) * gnorm_w.astype(jnp.float32)
    return y.astype(h.dtype) @ out_w


def shared_kv(h, positions, kv_norm_w, kv_w, k_norm_w):
    b, s, _ = h.shape
    u = rms_norm(h, kv_norm_w)
    k, v = jnp.split(u @ kv_w, 2, axis=-1)
    k = k.reshape(b, s, KV_HEADS, HEAD_DIM)
    v = v.reshape(b, s, KV_HEADS, HEAD_DIM)
    k = rope(rms_norm(k, k_norm_w), positions)
    return k, v


def swa_sink_attention(q, k, v, sinks):
    b, s = q.shape[:2]
    nb = s // BLOCK
    qb = q.reshape(b, nb, BLOCK, KV_HEADS, Q_PER_KV, HEAD_DIM)

    def with_prev(t):
        t = t.reshape(b, nb, BLOCK, KV_HEADS, HEAD_DIM)
        prev = jnp.pad(t, ((0, 0), (1, 0), (0, 0), (0, 0), (0, 0)))[:, :-1]
        return jnp.concatenate([prev, t], axis=2)

    kk, vv = with_prev(k), with_prev(v)
    scores = jnp.einsum('bnqhgd,bnkhd->bnhgqk', qb, kk).astype(jnp.float32) * (HEAD_DIM ** -0.5)
    blk = jnp.arange(nb)[:, None, None] * BLOCK
    q_pos = blk + jnp.arange(BLOCK)[None, :, None]
    k_pos = blk - BLOCK + jnp.arange(2 * BLOCK)[None, None, :]
    valid = (k_pos <= q_pos) & (q_pos - k_pos < WINDOW) & (k_pos >= 0)
    scores = jnp.where(valid[None, :, None, None], scores, -jnp.inf)
    sink = sinks.astype(jnp.float32).reshape(KV_HEADS, Q_PER_KV)[None, None, :, :, None, None]
    m = jnp.maximum(jnp.max(scores, axis=-1, keepdims=True), sink)
    e = jnp.exp(scores - m)
    probs = e / (jnp.sum(e, axis=-1, keepdims=True) + jnp.exp(sink - m))
    out = jnp.einsum('bnhgqk,bnkhd->bnqhgd', probs.astype(v.dtype), vv)
    return out.reshape(b, s, ATT_WIDTH)


def swa_layer(h, k, v, positions, norm_w, in_w, q_norm_w, sinks, out_w):
    b, s, _ = h.shape
    u = rms_norm(h, norm_w)
    q, gate = jnp.split(u @ in_w, 2, axis=-1)
    q = rope(rms_norm(q.reshape(b, s, ATT_HEADS, HEAD_DIM), q_norm_w), positions)
    o = swa_sink_attention(q, k, v, sinks)
    return (o * jax.nn.silu(gate)) @ out_w


def per_layer_embedding(h, p_i, norm_w, gate_w, proj_w):
    g = jax.nn.sigmoid((rms_norm(h, norm_w) @ gate_w).astype(jnp.float32))
    return (g * (p_i @ proj_w).astype(jnp.float32)).astype(h.dtype)


def setup_inputs(seed: int = 0) -> dict:
    key = jax.random.key(seed)
    ks = jax.random.split(key, 24)
    f32 = jnp.float32

    def nrm(k, shape, scale):
        return jax.random.normal(k, shape, f32) * scale

    dt0 = jnp.exp(jax.random.uniform(ks[7], (N_A_LAYERS, SSM_HEADS), f32, np.log(1e-3), np.log(1e-1)))
    offs = jax.random.randint(ks[3], (BATCH, 1), 0, 1024, jnp.int32)
    return {
        'x': nrm(ks[0], (BATCH, SEQ, D_MODEL), 1.0),
        'p': nrm(ks[1], (DEPTH, BATCH, SEQ, PLE_DIM), 1.0),
        'positions': (offs + jnp.arange(SEQ, dtype=jnp.int32)[None, :]).astype(jnp.int32),
        'ssm_norm_w': 1.0 + nrm(ks[2], (N_A_LAYERS, D_MODEL), 0.05),
        'ssm_in_w': nrm(ks[4], (N_A_LAYERS, D_MODEL, SSM_IN), D_MODEL ** -0.5),
        'ssm_conv_w': nrm(ks[5], (N_A_LAYERS, CONV_WIDTH, CONV_CH), CONV_WIDTH ** -0.5),
        'ssm_conv_b': nrm(ks[6], (N_A_LAYERS, CONV_CH), 0.02),
        'ssm_dt_bias': dt0 + jnp.log(-jnp.expm1(-dt0)),
        'ssm_a_log': jnp.log(jax.random.uniform(ks[8], (N_A_LAYERS, SSM_HEADS), f32, 1.0, 16.0)),
        'ssm_d': 1.0 + nrm(ks[9], (N_A_LAYERS, SSM_HEADS), 0.1),
        'ssm_gnorm_w': 1.0 + nrm(ks[10], (N_A_LAYERS, D_INNER), 0.05),
        'ssm_out_w': nrm(ks[11], (N_A_LAYERS, D_INNER, D_MODEL), D_INNER ** -0.5),
        'kv_norm_w': 1.0 + nrm(ks[12], (D_MODEL,), 0.05),
        'kv_w': nrm(ks[13], (D_MODEL, 2 * KV_WIDTH), D_MODEL ** -0.5),
        'k_norm_w': 1.0 + nrm(ks[14], (HEAD_DIM,), 0.05),
        'attn_norm_w': 1.0 + nrm(ks[15], (N_B_LAYERS, D_MODEL), 0.05),
        'attn_in_w': nrm(ks[16], (N_B_LAYERS, D_MODEL, 2 * ATT_WIDTH), D_MODEL ** -0.5),
        'q_norm_w': 1.0 + nrm(ks[17], (N_B_LAYERS, HEAD_DIM), 0.05),
        'attn_sinks': nrm(ks[18], (N_B_LAYERS, ATT_HEADS), 0.5),
        'attn_out_w': nrm(ks[19], (N_B_LAYERS, ATT_WIDTH, D_MODEL), ATT_WIDTH ** -0.5),
        'ple_norm_w': 1.0 + nrm(ks[20], (DEPTH, D_MODEL), 0.05),
        'ple_gate_w': nrm(ks[21], (DEPTH, D_MODEL, D_MODEL), D_MODEL ** -0.5),
        'ple_proj_w': nrm(ks[22], (DEPTH, PLE_DIM, D_MODEL), 0.5 * PLE_DIM ** -0.5),
    }


def reference(x, p, positions, ssm_norm_w, ssm_in_w, ssm_conv_w, ssm_conv_b, ssm_dt_bias, ssm_a_log,
              ssm_d, ssm_gnorm_w, ssm_out_w, kv_norm_w, kv_w, k_norm_w, attn_norm_w, attn_in_w,
              q_norm_w, attn_sinks, attn_out_w, ple_norm_w, ple_gate_w, ple_proj_w):
    h = x
    k_sh = None
    v_sh = None
    for i in range(DEPTH):
        if i < N_A_LAYERS:
            h = h + mamba2_mixer(h, ssm_norm_w[i], ssm_in_w[i], ssm_conv_w[i], ssm_conv_b[i],
                                 ssm_dt_bias[i], ssm_a_log[i], ssm_d[i], ssm_gnorm_w[i], ssm_out_w[i])
        else:
            if i == N_A_LAYERS:
                k_sh, v_sh = shared_kv(h, positions, kv_norm_w, kv_w, k_norm_w)
            j = i - N_A_LAYERS
            h = h + swa_layer(h, k_sh, v_sh, positions, attn_norm_w[j], attn_in_w[j], q_norm_w[j],
                              attn_sinks[j], attn_out_w[j])
        h = h + per_layer_embedding(h, p[i], ple_norm_w[i], ple_gate_w[i], ple_proj_w[i])
    return h
```

```python
import functools

import jax
import jax.numpy as jnp
import numpy as np
from jax import lax
from jax.experimental import pallas as pl
from jax.experimental.pallas import tpu as pltpu

F32 = jnp.float32
BF16 = jnp.bfloat16

D_MODEL = 1024
D_INNER = 2048
SSM_HEAD_DIM = 64
SSM_HEADS = 32
SSM_GROUPS = 4
SSM_HPG = 8
D_STATE = 128
CONV_WIDTH = 4
BC_WIDTH = SSM_GROUPS * D_STATE
CONV_CH = D_INNER + 2 * BC_WIDTH
CHUNK = 128
ATT_HEADS = 16
KV_HEADS = 4
HEAD_DIM = 64
HALF = HEAD_DIM // 2
Q_PER_KV = 4
ATT_WIDTH = 1024
KV_WIDTH = 256
BLOCK = 128
ROPE_THETA = 10000.0
PLE_DIM = 256
EPS = 1e-6

LANES = 128
TS = 256
NBLK = TS // BLOCK
VMEM_LIMIT = 56 * 1024 * 1024

_NT = (((1,), (1,)), ((), ()))


def _dot(a, b):
    return jnp.dot(a, b, preferred_element_type=F32)


def _rms(x, w):
    ms = jnp.mean(x * x, axis=-1, keepdims=True)
    return x * lax.rsqrt(ms + EPS) * w


def _silu(x):
    return x * jax.nn.sigmoid(x)


def _split2(x):
    hi = x.astype(BF16)
    lo = (x - hi.astype(F32)).astype(BF16)
    return hi, lo


def _ple(h2, p_tile, pnw_ref, pgw_ref, ppw_ref):
    u2 = _rms(h2, pnw_ref[...]).astype(BF16)
    gate = jax.nn.sigmoid(_dot(u2, pgw_ref[...]))
    proj = _dot(p_tile.astype(BF16), ppw_ref[...])
    return h2 + gate * proj


def _ssm_kernel(x_ref, p_ref, nw_ref, wz_ref, wxbc_ref, wdt_ref, cw_ref, cb_ref, dtb_ref, alog_ref,
                dsk_ref, gnw_ref, ow_ref, pnw_ref, pgw_ref, ppw_ref, o_ref,
                cbuf, xbc_s, y_s, y16_s, st_s):
    t = pl.program_id(1)

    @pl.when(t == 0)
    def _():
        cbuf[0:8, :] = jnp.zeros((8, CONV_CH), F32)
        st_s[...] = jnp.zeros(st_s.shape, F32)

    @pl.when(t > 0)
    def _():
        cbuf[0:8, :] = cbuf[TS:TS + 8, :]

    h = x_ref[0]
    u = _rms(h, nw_ref[...]).astype(BF16)
    cbuf[8:8 + TS, :] = _dot(u, wxbc_ref[...])
    dt_raw = _dot(u, wdt_ref[...]) + dtb_ref[...]
    dt = jnp.maximum(dt_raw, 0.0) + jnp.log1p(jnp.exp(-jnp.abs(dt_raw)))
    d_a = dt * (-jnp.exp(alog_ref[...]))

    for c0 in range(0, CONV_CH, 512):
        acc = cb_ref[:, c0:c0 + 512] + cw_ref[3:4, c0:c0 + 512] * cbuf[8:8 + TS, c0:c0 + 512]
        for k in range(CONV_WIDTH - 1):
            acc = acc + cw_ref[k:k + 1, c0:c0 + 512] * cbuf[5 + k:5 + k + TS, c0:c0 + 512]
        xbc_s[:, c0:c0 + 512] = _silu(acc)

    ii = lax.broadcasted_iota(jnp.int32, (CHUNK, CHUNK), 0)
    jj = lax.broadcasted_iota(jnp.int32, (CHUNK, CHUNK), 1)
    tri = jj <= ii
    lo_half = jj < SSM_HEAD_DIM
    ltri = tri.astype(F32)

    for c in range(TS // CHUNK):
        r0 = c * CHUNK
        da_c = d_a[r0:r0 + CHUNK]
        dt_c = dt[r0:r0 + CHUNK]
        acum = jnp.dot(ltri, da_c, precision=lax.Precision.HIGHEST, preferred_element_type=F32)
        acum_t = acum.T
        dt_t = dt_c.T
        last = acum[CHUNK - 1:CHUNK, :]
        e_in = jnp.exp(acum)
        w_out = jnp.exp(last - acum) * dt_c
        e_last = jnp.exp(last)
        for g in range(SSM_GROUPS):
            bm = xbc_s[r0:r0 + CHUNK, D_INNER + g * D_STATE:D_INNER + (g + 1) * D_STATE]
            cm = xbc_s[r0:r0 + CHUNK, D_INNER + BC_WIDTH + g * D_STATE:D_INNER + BC_WIDTH + (g + 1) * D_STATE]
            cb = lax.dot_general(cm.astype(BF16), bm.astype(BF16), _NT, preferred_element_type=F32)
            bm_t = bm.T.astype(BF16)
            for q in range(SSM_HPG // 2):
                pair = g * (SSM_HPG // 2) + q
                ha, hb = 2 * pair, 2 * pair + 1
                c0 = pair * LANES
                xs_pair = xbc_s[r0:r0 + CHUNK, c0:c0 + LANES]

                def m_diag(hh):
                    seg = acum[:, hh:hh + 1] - acum_t[hh:hh + 1, :]
                    dec = jnp.exp(jnp.where(tri, seg, -jnp.inf))
                    return (cb * dec * dt_t[hh:hh + 1, :]).astype(BF16)

                def m_off(hh):
                    return (cm * e_in[:, hh:hh + 1]).astype(BF16)

                lhs = jnp.concatenate([m_diag(ha), m_diag(hb), m_off(ha), m_off(hb)], axis=1)
                st = st_s[pair]
                xs16 = xs_pair.astype(BF16)
                st16 = st.astype(BF16)
                zero = jnp.zeros_like(xs16)
                rhs = jnp.concatenate([jnp.where(lo_half, xs16, zero), jnp.where(lo_half, zero, xs16),
                                       jnp.where(lo_half, st16, zero), jnp.where(lo_half, zero, st16)], axis=0)
                y_s[r0:r0 + CHUNK, c0:c0 + LANES] = _dot(lhs, rhs)

                w_pair = jnp.where(lo_half, w_out[:, ha:ha + 1], w_out[:, hb:hb + 1])
                xw = (xs_pair * w_pair).astype(BF16)
                el = jnp.where(lo_half[0:1, :], e_last[:, ha:ha + 1], e_last[:, hb:hb + 1])
                st_s[pair] = st * el + _dot(bm_t, xw)

    gw = D_INNER // SSM_GROUPS
    for g in range(SSM_GROUPS):
        c0 = g * gw
        y = y_s[:, c0:c0 + gw] + dsk_ref[:, c0:c0 + gw] * xbc_s[:, c0:c0 + gw]
        y = y * _silu(_dot(u, wz_ref[:, c0:c0 + gw]))
        y = y * lax.rsqrt(jnp.mean(y * y, axis=-1, keepdims=True) + EPS) * gnw_ref[:, c0:c0 + gw]
        y16_s[:, c0:c0 + gw] = y.astype(BF16)

    h2 = h + _dot(y16_s[...], ow_ref[...])
    o_ref[0] = _ple(h2, p_ref[0], pnw_ref, pgw_ref, ppw_ref)


def _const_spec(shape):
    nd = len(shape)
    return pl.BlockSpec(shape, lambda b, t: (0,) * nd, pipeline_mode=pl.Buffered(1))


def _ssm_layer(h, p_i, nw, wz, wxbc, wdt, cw, cb, dtb, alog, dsk, gnw, ow, pnw, pgw, ppw):
    bsz, seq, _ = h.shape
    weights = (nw, wz, wxbc, wdt, cw, cb, dtb, alog, dsk, gnw, ow, pnw, pgw, ppw)
    return pl.pallas_call(
        _ssm_kernel,
        grid=(bsz, seq // TS),
        in_specs=[pl.BlockSpec((1, TS, D_MODEL), lambda b, t: (b, t, 0)),
                  pl.BlockSpec((1, TS, PLE_DIM), lambda b, t: (b, t, 0))]
                 + [_const_spec(w.shape) for w in weights],
        out_specs=pl.BlockSpec((1, TS, D_MODEL), lambda b, t: (b, t, 0)),
        out_shape=jax.ShapeDtypeStruct(h.shape, F32),
        scratch_shapes=[pltpu.VMEM((TS + 8, CONV_CH), F32),
                        pltpu.VMEM((TS, CONV_CH), F32),
                        pltpu.VMEM((TS, D_INNER), F32),
                        pltpu.VMEM((TS, D_INNER), BF16),
                        pltpu.VMEM((SSM_HEADS // 2, D_STATE, LANES), F32)],
        compiler_params=pltpu.CompilerParams(dimension_semantics=("arbitrary", "arbitrary"),
                                             vmem_limit_bytes=VMEM_LIMIT),
        name="ssm_layer",
    )(h, p_i, *weights)


def _kv_kernel(x_ref, pos_ref, nw_ref, wk_ref, wv_ref, knw_ref, invf_ref, seg_ref,
               kt_ref, v_ref, cos_ref, sin_ref):
    u = _rms(x_ref[0], nw_ref[...]).astype(BF16)
    ang = pos_ref[0] * invf_ref[...]
    cos = jnp.cos(ang)
    sin = jnp.sin(ang)
    cos_ref[0] = cos
    sin_ref[0] = sin
    kk = _dot(u, wk_ref[...])
    k1, k2 = kk[:, :LANES], kk[:, LANES:]
    zhi, zlo = _split2(k1 * k1 + k2 * k2)
    ssum = _dot(zhi, seg_ref[...]) + _dot(zlo, seg_ref[...])
    rs = lax.rsqrt(ssum * (1.0 / HEAD_DIM) + EPS)
    n1 = k1 * rs * knw_ref[:, :LANES]
    n2 = k2 * rs * knw_ref[:, LANES:]
    kr = jnp.concatenate([n1 * cos - n2 * sin, n2 * cos + n1 * sin], axis=1)
    kt_ref[0] = kr.T.astype(BF16)
    v_ref[0] = _dot(u, wv_ref[...]).astype(BF16)


def _shared_kv(h, posf, nw, wk, wv, knw, invf, seg):
    bsz, seq, _ = h.shape
    consts = (nw, wk, wv, knw, invf, seg)
    return pl.pallas_call(
        _kv_kernel,
        grid=(bsz, seq // TS),
        in_specs=[pl.BlockSpec((1, TS, D_MODEL), lambda b, t: (b, t, 0)),
                  pl.BlockSpec((1, TS, 1), lambda b, t: (b, t, 0))]
                 + [_const_spec(w.shape) for w in consts],
        out_specs=[pl.BlockSpec((1, KV_WIDTH, TS), lambda b, t: (b, 0, t)),
                   pl.BlockSpec((1, TS, 2 * KV_WIDTH), lambda b, t: (b, t, 0)),
                   pl.BlockSpec((1, TS, LANES), lambda b, t: (b, t, 0)),
                   pl.BlockSpec((1, TS, LANES), lambda b, t: (b, t, 0))],
        out_shape=[jax.ShapeDtypeStruct((bsz, KV_WIDTH, seq), BF16),
                   jax.ShapeDtypeStruct((bsz, seq, 2 * KV_WIDTH), BF16),
                   jax.ShapeDtypeStruct((bsz, seq, LANES), F32),
                   jax.ShapeDtypeStruct((bsz, seq, LANES), F32)],
        compiler_params=pltpu.CompilerParams(dimension_semantics=("arbitrary", "arbitrary"),
                                             vmem_limit_bytes=VMEM_LIMIT),
        name="shared_kv",
    )(h, posf, *consts)


def _attn_kernel(sink_ref, x_ref, p_ref, cos_ref, sin_ref, ktc_ref, ktp_ref, vc_ref, vp_ref,
                 nw_ref, wq_ref, wg_ref, qnw_ref, seg_ref, ow_ref, pnw_ref, pgw_ref, ppw_ref, o_ref,
                 q16_s, krhs_s, o_s):
    t = pl.program_id(1)
    qw = ATT_WIDTH // 2

    @pl.when(t == 0)
    def _():
        krhs_s[...] = jnp.zeros(krhs_s.shape, BF16)

    h = x_ref[0]
    u = _rms(h, nw_ref[...]).astype(BF16)
    q1 = _dot(u, wq_ref[:, :qw])
    q2 = _dot(u, wq_ref[:, qw:])
    zhi, zlo = _split2(q1 * q1 + q2 * q2)
    ssum = _dot(zhi, seg_ref[...]) + _dot(zlo, seg_ref[...])
    rs = lax.rsqrt(ssum * (1.0 / HEAD_DIM) + EPS)
    n1 = q1 * rs * qnw_ref[:, :qw]
    n2 = q2 * rs * qnw_ref[:, qw:]
    cos = jnp.concatenate([cos_ref[0]] * (qw // LANES), axis=1)
    sin = jnp.concatenate([sin_ref[0]] * (qw // LANES), axis=1)
    scale = HEAD_DIM ** -0.5
    q16_s[:, :qw] = ((n1 * cos - n2 * sin) * scale).astype(BF16)
    q16_s[:, qw:] = ((n2 * cos + n1 * sin) * scale).astype(BF16)

    ii = lax.broadcasted_iota(jnp.int32, (BLOCK, BLOCK), 0)
    jj = lax.broadcasted_iota(jnp.int32, (BLOCK, BLOCK), 1)
    own_ok = jj <= ii
    prev_ok = jj > ii
    lo_half = lax.broadcasted_iota(jnp.int32, (2 * BLOCK, LANES), 1) < HEAD_DIM

    for qb in range(NBLK):
        r0 = qb * BLOCK
        if qb == 0:
            kprev = ktp_ref[0]
            vprev = vp_ref[0]
            first_off = jnp.where(t > 0, 0, BLOCK)
            valid = jnp.concatenate([jj > ii + first_off, own_ok], axis=1)
        else:
            kprev = ktc_ref[0, :, r0 - BLOCK:r0]
            vprev = vc_ref[0, r0 - BLOCK:r0, :]
            valid = jnp.concatenate([prev_ok, own_ok], axis=1)
        kown = ktc_ref[0, :, r0:r0 + BLOCK]
        vown = vc_ref[0, r0:r0 + BLOCK, :]
        for g in range(KV_HEADS):
            for hl in range(Q_PER_KV):
                for half in range(2):
                    src = half * LANES + g * HALF
                    dst = half * LANES + hl * HALF
                    krhs_s[dst:dst + HALF, hl * 2 * BLOCK:hl * 2 * BLOCK + BLOCK] = kprev[src:src + HALF, :]
                    krhs_s[dst:dst + HALF, hl * 2 * BLOCK + BLOCK:(hl + 1) * 2 * BLOCK] = kown[src:src + HALF, :]
            lhs = jnp.concatenate([q16_s[r0:r0 + BLOCK, g * LANES:(g + 1) * LANES],
                                   q16_s[r0:r0 + BLOCK, qw + g * LANES:qw + (g + 1) * LANES]], axis=1)
            s_all = _dot(lhs, krhs_s[...])
            vg = jnp.concatenate([vprev[:, g * LANES:(g + 1) * LANES], vown[:, g * LANES:(g + 1) * LANES]], axis=0)
            zero = jnp.zeros_like(vg)
            vbd = jnp.concatenate([jnp.where(lo_half, vg, zero), jnp.where(lo_half, zero, vg)], axis=0)
            probs = []
            for hl in range(Q_PER_KV):
                sink = sink_ref[g * Q_PER_KV + hl]
                s = jnp.where(valid, s_all[:, hl * 2 * BLOCK:(hl + 1) * 2 * BLOCK], -jnp.inf)
                m = jnp.maximum(jnp.max(s, axis=-1, keepdims=True), sink)
                e = jnp.exp(s - m)
                den = jnp.sum(e, axis=-1, keepdims=True) + jnp.exp(sink - m)
                probs.append((e / den).astype(BF16))
            for pr in range(Q_PER_KV // 2):
                lhs_p = jnp.concatenate([probs[2 * pr], probs[2 * pr + 1]], axis=1)
                c0 = (g * 2 + pr) * LANES
                o_s[r0:r0 + BLOCK, c0:c0 + LANES] = _dot(lhs_p, vbd)

    og = o_s[...] * _silu(_dot(u, wg_ref[...]))
    h2 = h + _dot(og.astype(BF16), ow_ref[...])
    o_ref[0] = _ple(h2, p_ref[0], pnw_ref, pgw_ref, ppw_ref)


def _attn_layer(h, p_i, sinks, cos, sin, kt, v2, nw, wq, wg, qnw, seg, ow, pnw, pgw, ppw):
    bsz, seq, _ = h.shape
    consts = (nw, wq, wg, qnw, seg, ow, pnw, pgw, ppw)
    prev_blk = lambda t: jnp.maximum(t * NBLK - 1, 0)
    return pl.pallas_call(
        _attn_kernel,
        grid=(bsz, seq // TS),
        in_specs=[pl.BlockSpec(memory_space=pltpu.SMEM),
                  pl.BlockSpec((1, TS, D_MODEL), lambda b, t: (b, t, 0)),
                  pl.BlockSpec((1, TS, PLE_DIM), lambda b, t: (b, t, 0)),
                  pl.BlockSpec((1, TS, LANES), lambda b, t: (b, t, 0)),
                  pl.BlockSpec((1, TS, LANES), lambda b, t: (b, t, 0)),
                  pl.BlockSpec((1, KV_WIDTH, TS), lambda b, t: (b, 0, t)),
                  pl.BlockSpec((1, KV_WIDTH, BLOCK), lambda b, t: (b, 0, prev_blk(t))),
                  pl.BlockSpec((1, TS, 2 * KV_WIDTH), lambda b, t: (b, t, 0)),
                  pl.BlockSpec((1, BLOCK, 2 * KV_WIDTH), lambda b, t: (b, prev_blk(t), 0))]
                 + [_const_spec(w.shape) for w in consts],
        out_specs=pl.BlockSpec((1, TS, D_MODEL), lambda b, t: (b, t, 0)),
        out_shape=jax.ShapeDtypeStruct(h.shape, F32),
        scratch_shapes=[pltpu.VMEM((TS, ATT_WIDTH), BF16),
                        pltpu.VMEM((2 * LANES, Q_PER_KV * 2 * BLOCK), BF16),
                        pltpu.VMEM((TS, ATT_WIDTH), F32)],
        compiler_params=pltpu.CompilerParams(dimension_semantics=("arbitrary", "arbitrary"),
                                             vmem_limit_bytes=VMEM_LIMIT),
        name="attn_layer",
    )(sinks, h, p_i, cos, sin, kt, kt, v2, v2, *consts)


def _row(v):
    return v.reshape(1, -1).astype(F32)


def _pad_lanes(v):
    return jnp.pad(v, [(0, 0)] * (v.ndim - 1) + [(0, LANES - v.shape[-1])])


def _halves_perm(n_heads):
    d = np.arange(HALF)
    first = (np.arange(n_heads)[:, None] * HEAD_DIM + d[None, :]).reshape(-1)
    return np.concatenate([first, first + HALF])


def _segment_ones(width):
    idx = np.arange(width) // HALF
    return jnp.asarray(idx[:, None] == idx[None, :], dtype=BF16)


def kernel(x, p, positions, ssm_norm_w, ssm_in_w, ssm_conv_w, ssm_conv_b, ssm_dt_bias, ssm_a_log, ssm_d,
           ssm_gnorm_w, ssm_out_w, kv_norm_w, kv_w, k_norm_w, attn_norm_w, attn_in_w, q_norm_w, attn_sinks,
           attn_out_w, ple_norm_w, ple_gate_w, ple_proj_w):
    n_a = ssm_in_w.shape[0]
    n_b = attn_in_w.shape[0]
    h = x
    for i in range(n_a):
        in_w = ssm_in_w[i]
        h = _ssm_layer(
            h, p[i], _row(ssm_norm_w[i]),
            in_w[:, :D_INNER].astype(BF16),
            in_w[:, D_INNER:D_INNER + CONV_CH].astype(BF16),
            _pad_lanes(in_w[:, D_INNER + CONV_CH:]).astype(BF16),
            ssm_conv_w[i].astype(F32), _row(ssm_conv_b[i]),
            _pad_lanes(_row(ssm_dt_bias[i])), _pad_lanes(_row(ssm_a_log[i])),
            _row(jnp.repeat(ssm_d[i], SSM_HEAD_DIM)), _row(ssm_gnorm_w[i]),
            ssm_out_w[i].astype(BF16),
            _row(ple_norm_w[i]), ple_gate_w[i].astype(BF16), ple_proj_w[i].astype(BF16))

    inv_freq = ROPE_THETA ** (-(jnp.arange(HALF, dtype=F32) * 2.0 / HEAD_DIM))
    kperm = _halves_perm(KV_HEADS)
    qperm = _halves_perm(ATT_HEADS)
    posf = positions.astype(F32)[..., None]
    wv = kv_w[:, KV_WIDTH:].reshape(D_MODEL, KV_HEADS, 1, HEAD_DIM)
    wv = jnp.broadcast_to(wv, (D_MODEL, KV_HEADS, 2, HEAD_DIM)).reshape(D_MODEL, 2 * KV_WIDTH)
    kt, v2, cos, sin = _shared_kv(
        h, posf, _row(kv_norm_w), kv_w[:, :KV_WIDTH][:, kperm].astype(BF16), wv.astype(BF16),
        _row(jnp.tile(k_norm_w.reshape(2, HALF), (1, KV_HEADS))),
        _row(jnp.tile(inv_freq, LANES // HALF)), _segment_ones(LANES))

    seg_q = _segment_ones(ATT_WIDTH // 2)
    for j in range(n_b):
        i = n_a + j
        in_w = attn_in_w[j]
        h = _attn_layer(
            h, p[i], attn_sinks[j].astype(F32), cos, sin, kt, v2, _row(attn_norm_w[j]),
            in_w[:, :ATT_WIDTH][:, qperm].astype(BF16), in_w[:, ATT_WIDTH:].astype(BF16),
            _row(jnp.tile(q_norm_w[j].reshape(2, HALF), (1, ATT_HEADS))), seg_q,
            attn_out_w[j].astype(BF16),
            _row(ple_norm_w[i]), ple_gate_w[i].astype(BF16), ple_proj_w[i].astype(BF16))
    return h
```

```python
import jax
import jax.numpy as jnp
import numpy as np
from jax import lax
from jax.experimental import pallas as pl
from jax.experimental.pallas import tpu as pltpu

F32 = jnp.float32
BF16 = jnp.bfloat16

D_MODEL = 1024
D_INNER = 2048
SSM_HEAD_DIM = 64
SSM_HEADS = 32
SSM_GROUPS = 4
SSM_HPG = 8
D_STATE = 128
CONV_WIDTH = 4
BC_WIDTH = SSM_GROUPS * D_STATE
CONV_CH = D_INNER + 2 * BC_WIDTH
CHUNK = 128
ATT_HEADS = 16
KV_HEADS = 4
HEAD_DIM = 64
HALF = HEAD_DIM // 2
Q_PER_KV = 4
ATT_WIDTH = 1024
KV_WIDTH = 256
BLOCK = 128
ROPE_THETA = 10000.0
PLE_DIM = 256
EPS = 1e-6

LANES = 128
TS = 256
NBLK = TS // BLOCK
NCH = TS // CHUNK
WCOL = 512
GW = D_INNER // SSM_GROUPS
PAIRS = SSM_HEADS // 2
PPG = SSM_HPG // 2
VMEM_LIMIT = 56 * 1024 * 1024

_NT = (((1,), (1,)), ((), ()))


def _dot(a, b):
    return jnp.dot(a, b, preferred_element_type=F32)


def _dotw(a, w_ref):
    return jnp.concatenate([_dot(a, w_ref[c]) for c in range(w_ref.shape[0])], axis=1)


def _rms(x, w):
    ms = jnp.mean(x * x, axis=-1, keepdims=True)
    return x * lax.rsqrt(ms + EPS) * w


def _sigmoid(x):
    return 0.5 * jnp.tanh(0.5 * x) + 0.5


def _silu(x):
    hx = 0.5 * x
    return hx * jnp.tanh(hx) + hx


def _split2(x):
    hi = x.astype(BF16)
    lo = (x - hi.astype(F32)).astype(BF16)
    return hi, lo


def _ple(h2, p_tile, pnw_ref, pgw_ref, ppw_ref):
    u2 = _rms(h2, pnw_ref[...]).astype(BF16)
    gate = _sigmoid(_dotw(u2, pgw_ref))
    proj = _dotw(p_tile.astype(BF16), ppw_ref)
    return h2 + gate * proj


def _ssm_kernel(x_ref, p_ref, nw_ref, wz_ref, wxbc_ref, wdt_ref, cw_ref, cb_ref, dtb_ref, alog_ref,
                dsk_ref, gnw_ref, ow_ref, pnw_ref, pgw_ref, ppw_ref, o_ref,
                cbuf, u_s, zs_s, y16_s, h2_s, st_s):
    t = pl.program_id(1)

    @pl.when(t == 0)
    def _():
        cbuf[0:8, :] = jnp.zeros((8, CONV_CH), F32)
        st_s[...] = jnp.zeros(st_s.shape, F32)

    @pl.when(t > 0)
    def _():
        cbuf[0:8, :] = cbuf[TS:TS + 8, :]

    u_s[...] = _rms(x_ref[0], nw_ref[...]).astype(BF16)
    dt_raw = _dot(u_s[...], wdt_ref[...]) + dtb_ref[...]
    dt = jnp.maximum(dt_raw, 0.0) + jnp.log1p(jnp.exp(-jnp.abs(dt_raw)))
    d_a = dt * (-jnp.exp(alog_ref[...]))

    ii = lax.broadcasted_iota(jnp.int32, (CHUNK, LANES), 0)
    ll = lax.broadcasted_iota(jnp.int32, (CHUNK, LANES), 1)
    lo_half = ll < SSM_HEAD_DIM
    jlo = jnp.where(lo_half, ll, ll - SSM_HEAD_DIM)
    mask_lo = jlo <= ii
    mask_hi = jlo + SSM_HEAD_DIM <= ii
    ltri = (ll <= ii).astype(F32)
    lo_row = lax.broadcasted_iota(jnp.int32, (PAIRS, LANES), 1) < SSM_HEAD_DIM
    lo_k = lax.broadcasted_iota(jnp.int32, (SSM_HEAD_DIM, LANES), 1) < SSM_HEAD_DIM

    tabs = []
    for c in range(NCH):
        r0 = c * CHUNK
        dt_c = dt[r0:r0 + CHUNK]
        acum = jnp.dot(ltri, d_a[r0:r0 + CHUNK], precision=lax.Precision.HIGHEST,
                       preferred_element_type=F32)
        acum_t = acum.T
        dt_t = dt_c.T
        ev, od = acum_t[:PAIRS], acum_t[PAIRS:2 * PAIRS]
        dev, dod = dt_t[:PAIRS], dt_t[PAIRS:2 * PAIRS]
        tabs.append(dict(
            acum=acum,
            w_out=jnp.exp(acum[CHUNK - 1:CHUNK, :] - acum) * dt_c,
            a_lo=jnp.where(lo_row, ev, pltpu.roll(od, SSM_HEAD_DIM, 1)),
            a_hi=jnp.where(lo_row, pltpu.roll(ev, SSM_HEAD_DIM, 1), od),
            d_lo=jnp.where(lo_row, dev, pltpu.roll(dod, SSM_HEAD_DIM, 1)),
            d_hi=jnp.where(lo_row, pltpu.roll(dev, SSM_HEAD_DIM, 1), dod)))

    def pair_expand(v, pair):
        idx = jnp.where(lo_half, pair, PAIRS + pair)
        return jnp.take_along_axis(v, idx, axis=1)

    def proj_item(c, j):
        r0 = c * CHUNK
        cbuf[8 + r0:8 + r0 + CHUNK, j * WCOL:(j + 1) * WCOL] = _dot(u_s[r0:r0 + CHUNK, :], wxbc_ref[j])

    def z_item(c, g):
        r0 = c * CHUNK
        zs_s[r0:r0 + CHUNK, g * GW:(g + 1) * GW] = _silu(_dot(u_s[r0:r0 + CHUNK, :], wz_ref[g]))

    def conv(r0, c0, width):
        acc = cb_ref[:, c0:c0 + width] + cw_ref[3:4, c0:c0 + width] * cbuf[8 + r0:8 + r0 + CHUNK, c0:c0 + width]
        for k in range(CONV_WIDTH - 1):
            acc = acc + cw_ref[k:k + 1, c0:c0 + width] * cbuf[5 + k + r0:5 + k + r0 + CHUNK, c0:c0 + width]
        return _silu(acc)

    def ssd_item(c, g):
        r0, gc, tab = c * CHUNK, g * GW, tabs[c]
        xs = conv(r0, gc, GW)
        bm = conv(r0, D_INNER + g * D_STATE, D_STATE)
        cm = conv(r0, D_INNER + BC_WIDTH + g * D_STATE, D_STATE)
        cm16, bm16 = cm.astype(BF16), bm.astype(BF16)
        h0, h1 = bm16[:SSM_HEAD_DIM], bm16[SSM_HEAD_DIM:]
        cb2 = lax.dot_general(cm16, jnp.concatenate([h0, h0, h1, h1], axis=0), _NT,
                              preferred_element_type=F32)
        cb_lo, cb_hi = cb2[:, :LANES], cb2[:, LANES:]
        a_exp = [pair_expand(tab["acum"], g * PPG + q) for q in range(PPG)]
        w_exp = jnp.concatenate([pair_expand(tab["w_out"], g * PPG + q) for q in range(PPG)], axis=1)
        a_all = jnp.concatenate(a_exp, axis=1)
        st = st_s[g]
        y_off = _dot(cm16, st.astype(BF16))
        st_s[g] = st * jnp.exp(a_all[CHUNK - 1:CHUNK, :]) + _dot(bm.T.astype(BF16), (xs * w_exp).astype(BF16))

        y_diag = []
        for q in range(PPG):
            pair = g * PPG + q
            t_lo = cb_lo * jnp.exp(jnp.where(mask_lo, a_exp[q] - tab["a_lo"][pair:pair + 1, :], -jnp.inf))
            t_hi = cb_hi * jnp.exp(jnp.where(mask_hi, a_exp[q] - tab["a_hi"][pair:pair + 1, :], -jnp.inf))
            lhs = jnp.concatenate([(t_lo * tab["d_lo"][pair:pair + 1, :]).astype(BF16),
                                   (t_hi * tab["d_hi"][pair:pair + 1, :]).astype(BF16)], axis=1)
            xs16 = xs[:, q * LANES:(q + 1) * LANES].astype(BF16)
            x0, x1 = xs16[:SSM_HEAD_DIM], xs16[SSM_HEAD_DIM:]
            zero = jnp.zeros_like(x0)
            rhs = jnp.concatenate([jnp.where(lo_k, x0, zero), jnp.where(lo_k, zero, x0),
                                   jnp.where(lo_k, x1, zero), jnp.where(lo_k, zero, x1)], axis=0)
            y_diag.append(_dot(lhs, rhs))
        y = jnp.concatenate(y_diag, axis=1) + jnp.exp(a_all) * y_off + dsk_ref[:, gc:gc + GW] * xs
        y = y * zs_s[r0:r0 + CHUNK, gc:gc + GW]
        y = y * lax.rsqrt(jnp.mean(y * y, axis=-1, keepdims=True) + EPS) * gnw_ref[:, gc:gc + GW]
        y16_s[r0:r0 + CHUNK, gc:gc + GW] = y.astype(BF16)

    def out_item(c, k):
        r0 = c * CHUNK
        h2_s[r0:r0 + CHUNK, k * WCOL:(k + 1) * WCOL] = (
            x_ref[0, r0:r0 + CHUNK, k * WCOL:(k + 1) * WCOL] + _dot(y16_s[r0:r0 + CHUNK, :], ow_ref[k]))

    def ple_item(c):
        r0 = c * CHUNK
        o_ref[0, r0:r0 + CHUNK, :] = _ple(h2_s[r0:r0 + CHUNK, :], p_ref[0, r0:r0 + CHUNK, :],
                                          pnw_ref, pgw_ref, ppw_ref)

    bc = (D_INNER // WCOL, D_INNER // WCOL + 1)
    for j in (*bc, 0, 1):
        proj_item(0, j)
    z_item(0, 0)
    for c in range(NCH):
        more = c + 1 < NCH
        ssd_item(c, 0)
        proj_item(c, 2)
        z_item(c, 1)
        if c > 0:
            out_item(c - 1, 0)
        ssd_item(c, 1)
        proj_item(c, 3)
        z_item(c, 2)
        if c > 0:
            out_item(c - 1, 1)
        ssd_item(c, 2)
        z_item(c, 3)
        if more:
            proj_item(c + 1, bc[0])
            proj_item(c + 1, bc[1])
        if c > 0:
            ple_item(c - 1)
        ssd_item(c, 3)
        if more:
            proj_item(c + 1, 0)
            z_item(c + 1, 0)
            proj_item(c + 1, 1)
    out_item(NCH - 1, 0)
    out_item(NCH - 1, 1)
    ple_item(NCH - 1)


def _const_spec(shape):
    nd = len(shape)
    return pl.BlockSpec(shape, lambda b, t: (0,) * nd, pipeline_mode=pl.Buffered(1))


def _ssm_layer(h, p_i, nw, wz, wxbc, wdt, cw, cb, dtb, alog, dsk, gnw, ow, pnw, pgw, ppw):
    bsz, seq, _ = h.shape
    weights = (nw, wz, wxbc, wdt, cw, cb, dtb, alog, dsk, gnw, ow, pnw, pgw, ppw)
    return pl.pallas_call(
        _ssm_kernel,
        grid=(bsz, seq // TS),
        in_specs=[pl.BlockSpec((1, TS, D_MODEL), lambda b, t: (b, t, 0)),
                  pl.BlockSpec((1, TS, PLE_DIM), lambda b, t: (b, t, 0))]
                 + [_const_spec(w.shape) for w in weights],
        out_specs=pl.BlockSpec((1, TS, D_MODEL), lambda b, t: (b, t, 0)),
        out_shape=jax.ShapeDtypeStruct(h.shape, F32),
        scratch_shapes=[pltpu.VMEM((TS + 8, CONV_CH), F32),
                        pltpu.VMEM((TS, D_MODEL), BF16),
                        pltpu.VMEM((TS, D_INNER), F32),
                        pltpu.VMEM((TS, D_INNER), BF16),
                        pltpu.VMEM((TS, D_MODEL), F32),
                        pltpu.VMEM((SSM_GROUPS, D_STATE, GW), F32)],
        compiler_params=pltpu.CompilerParams(dimension_semantics=("arbitrary", "arbitrary"),
                                             vmem_limit_bytes=VMEM_LIMIT),
        name="ssm_layer",
    )(h, p_i, *weights)


def _kv_kernel(x_ref, pos_ref, nw_ref, wk_ref, wv_ref, knw_ref, invf_ref, seg_ref,
               kt_ref, v_ref, cos_ref, sin_ref):
    u = _rms(x_ref[0], nw_ref[...]).astype(BF16)
    ang = pos_ref[0] * invf_ref[...]
    cos = jnp.cos(ang)
    sin = jnp.sin(ang)
    cos_ref[0] = cos
    sin_ref[0] = sin
    kk = _dot(u, wk_ref[...])
    k1, k2 = kk[:, :LANES], kk[:, LANES:]
    zhi, zlo = _split2(k1 * k1 + k2 * k2)
    ssum = _dot(zhi, seg_ref[...]) + _dot(zlo, seg_ref[...])
    rs = lax.rsqrt(ssum * (1.0 / HEAD_DIM) + EPS)
    n1 = k1 * rs * knw_ref[:, :LANES]
    n2 = k2 * rs * knw_ref[:, LANES:]
    kr = jnp.concatenate([n1 * cos - n2 * sin, n2 * cos + n1 * sin], axis=1)
    kt_ref[0] = kr.T.astype(BF16)
    v_ref[0] = _dot(u, wv_ref[...]).astype(BF16)


def _shared_kv(h, posf, nw, wk, wv, knw, invf, seg):
    bsz, seq, _ = h.shape
    consts = (nw, wk, wv, knw, invf, seg)
    return pl.pallas_call(
        _kv_kernel,
        grid=(bsz, seq // TS),
        in_specs=[pl.BlockSpec((1, TS, D_MODEL), lambda b, t: (b, t, 0)),
                  pl.BlockSpec((1, TS, 1), lambda b, t: (b, t, 0))]
                 + [_const_spec(w.shape) for w in consts],
        out_specs=[pl.BlockSpec((1, KV_WIDTH, TS), lambda b, t: (b, 0, t)),
                   pl.BlockSpec((1, TS, 2 * KV_WIDTH), lambda b, t: (b, t, 0)),
                   pl.BlockSpec((1, TS, LANES), lambda b, t: (b, t, 0)),
                   pl.BlockSpec((1, TS, LANES), lambda b, t: (b, t, 0))],
        out_shape=[jax.ShapeDtypeStruct((bsz, KV_WIDTH, seq), BF16),
                   jax.ShapeDtypeStruct((bsz, seq, 2 * KV_WIDTH), BF16),
                   jax.ShapeDtypeStruct((bsz, seq, LANES), F32),
                   jax.ShapeDtypeStruct((bsz, seq, LANES), F32)],
        compiler_params=pltpu.CompilerParams(dimension_semantics=("arbitrary", "arbitrary"),
                                             vmem_limit_bytes=VMEM_LIMIT),
        name="shared_kv",
    )(h, posf, *consts)


def _attn_kernel(sink_ref, x_ref, p_ref, cos_ref, sin_ref, ktc_ref, ktp_ref, vc_ref, vp_ref,
                 nw_ref, wq_ref, wg_ref, qnw_ref, seg_ref, ow_ref, pnw_ref, pgw_ref, ppw_ref, o_ref,
                 q16_s, krhs_s, o_s):
    t = pl.program_id(1)
    qw = ATT_WIDTH // 2

    @pl.when(t == 0)
    def _():
        krhs_s[...] = jnp.zeros(krhs_s.shape, BF16)

    h = x_ref[0]
    u = _rms(h, nw_ref[...]).astype(BF16)
    q1 = _dot(u, wq_ref[0])
    q2 = _dot(u, wq_ref[1])
    zhi, zlo = _split2(q1 * q1 + q2 * q2)
    ssum = _dot(zhi, seg_ref[...]) + _dot(zlo, seg_ref[...])
    rs = lax.rsqrt(ssum * (1.0 / HEAD_DIM) + EPS)
    n1 = q1 * rs * qnw_ref[:, :qw]
    n2 = q2 * rs * qnw_ref[:, qw:]
    cos = jnp.concatenate([cos_ref[0]] * (qw // LANES), axis=1)
    sin = jnp.concatenate([sin_ref[0]] * (qw // LANES), axis=1)
    scale = HEAD_DIM ** -0.5
    q16_s[:, :qw] = ((n1 * cos - n2 * sin) * scale).astype(BF16)
    q16_s[:, qw:] = ((n2 * cos + n1 * sin) * scale).astype(BF16)

    ii = lax.broadcasted_iota(jnp.int32, (BLOCK, BLOCK), 0)
    jj = lax.broadcasted_iota(jnp.int32, (BLOCK, BLOCK), 1)
    own_ok = jj <= ii
    prev_ok = jj > ii
    lo_half = lax.broadcasted_iota(jnp.int32, (2 * BLOCK, LANES), 1) < HEAD_DIM

    for qb in range(NBLK):
        r0 = qb * BLOCK
        if qb == 0:
            kprev = ktp_ref[0]
            vprev = vp_ref[0]
            first_off = jnp.where(t > 0, 0, BLOCK)
            valid = jnp.concatenate([jj > ii + first_off, own_ok], axis=1)
        else:
            kprev = ktc_ref[0, :, r0 - BLOCK:r0]
            vprev = vc_ref[0, r0 - BLOCK:r0, :]
            valid = jnp.concatenate([prev_ok, own_ok], axis=1)
        kown = ktc_ref[0, :, r0:r0 + BLOCK]
        vown = vc_ref[0, r0:r0 + BLOCK, :]
        for g in range(KV_HEADS):
            for hl in range(Q_PER_KV):
                for half in range(2):
                    src = half * LANES + g * HALF
                    dst = half * LANES + hl * HALF
                    krhs_s[dst:dst + HALF, hl * 2 * BLOCK:hl * 2 * BLOCK + BLOCK] = kprev[src:src + HALF, :]
                    krhs_s[dst:dst + HALF, hl * 2 * BLOCK + BLOCK:(hl + 1) * 2 * BLOCK] = kown[src:src + HALF, :]
            lhs = jnp.concatenate([q16_s[r0:r0 + BLOCK, g * LANES:(g + 1) * LANES],
                                   q16_s[r0:r0 + BLOCK, qw + g * LANES:qw + (g + 1) * LANES]], axis=1)
            s_all = _dot(lhs, krhs_s[...])
            vg = jnp.concatenate([vprev[:, g * LANES:(g + 1) * LANES], vown[:, g * LANES:(g + 1) * LANES]], axis=0)
            zero = jnp.zeros_like(vg)
            vbd = jnp.concatenate([jnp.where(lo_half, vg, zero), jnp.where(lo_half, zero, vg)], axis=0)
            probs = []
            for hl in range(Q_PER_KV):
                sink = sink_ref[g * Q_PER_KV + hl]
                s = jnp.where(valid, s_all[:, hl * 2 * BLOCK:(hl + 1) * 2 * BLOCK], -jnp.inf)
                m = jnp.maximum(jnp.max(s, axis=-1, keepdims=True), sink)
                e = jnp.exp(s - m)
                den = jnp.sum(e, axis=-1, keepdims=True) + jnp.exp(sink - m)
                probs.append((e / den).astype(BF16))
            for pr in range(Q_PER_KV // 2):
                lhs_p = jnp.concatenate([probs[2 * pr], probs[2 * pr + 1]], axis=1)
                c0 = (g * 2 + pr) * LANES
                o_s[r0:r0 + BLOCK, c0:c0 + LANES] = _dot(lhs_p, vbd)

    og = o_s[...] * _silu(_dotw(u, wg_ref))
    h2 = h + _dotw(og.astype(BF16), ow_ref)
    o_ref[0] = _ple(h2, p_ref[0], pnw_ref, pgw_ref, ppw_ref)


def _attn_layer(h, p_i, sinks, cos, sin, kt, v2, nw, wq, wg, qnw, seg, ow, pnw, pgw, ppw):
    bsz, seq, _ = h.shape
    consts = (nw, wq, wg, qnw, seg, ow, pnw, pgw, ppw)
    prev_blk = lambda t: jnp.maximum(t * NBLK - 1, 0)
    return pl.pallas_call(
        _attn_kernel,
        grid=(bsz, seq // TS),
        in_specs=[pl.BlockSpec(memory_space=pltpu.SMEM),
                  pl.BlockSpec((1, TS, D_MODEL), lambda b, t: (b, t, 0)),
                  pl.BlockSpec((1, TS, PLE_DIM), lambda b, t: (b, t, 0)),
                  pl.BlockSpec((1, TS, LANES), lambda b, t: (b, t, 0)),
                  pl.BlockSpec((1, TS, LANES), lambda b, t: (b, t, 0)),
                  pl.BlockSpec((1, KV_WIDTH, TS), lambda b, t: (b, 0, t)),
                  pl.BlockSpec((1, KV_WIDTH, BLOCK), lambda b, t: (b, 0, prev_blk(t))),
                  pl.BlockSpec((1, TS, 2 * KV_WIDTH), lambda b, t: (b, t, 0)),
                  pl.BlockSpec((1, BLOCK, 2 * KV_WIDTH), lambda b, t: (b, prev_blk(t), 0))]
                 + [_const_spec(w.shape) for w in consts],
        out_specs=pl.BlockSpec((1, TS, D_MODEL), lambda b, t: (b, t, 0)),
        out_shape=jax.ShapeDtypeStruct(h.shape, F32),
        scratch_shapes=[pltpu.VMEM((TS, ATT_WIDTH), BF16),
                        pltpu.VMEM((2 * LANES, Q_PER_KV * 2 * BLOCK), BF16),
                        pltpu.VMEM((TS, ATT_WIDTH), F32)],
        compiler_params=pltpu.CompilerParams(dimension_semantics=("arbitrary", "arbitrary"),
                                             vmem_limit_bytes=VMEM_LIMIT),
        name="attn_layer",
    )(sinks, h, p_i, cos, sin, kt, kt, v2, v2, *consts)


def _row(v):
    return v.reshape(1, -1).astype(F32)


def _pad_lanes(v):
    return jnp.pad(v, [(0, 0)] * (v.ndim - 1) + [(0, LANES - v.shape[-1])])


def _wchunks(w):
    k, n = w.shape
    return w.astype(BF16).reshape(k, n // WCOL, WCOL).transpose(1, 0, 2)


def _halves_perm(n_heads):
    d = np.arange(HALF)
    first = (np.arange(n_heads)[:, None] * HEAD_DIM + d[None, :]).reshape(-1)
    return np.concatenate([first, first + HALF])


def _segment_ones(width):
    idx = np.arange(width) // HALF
    return jnp.asarray(idx[:, None] == idx[None, :], dtype=BF16)


def kernel(x, p, positions, ssm_norm_w, ssm_in_w, ssm_conv_w, ssm_conv_b, ssm_dt_bias, ssm_a_log, ssm_d,
           ssm_gnorm_w, ssm_out_w, kv_norm_w, kv_w, k_norm_w, attn_norm_w, attn_in_w, q_norm_w, attn_sinks,
           attn_out_w, ple_norm_w, ple_gate_w, ple_proj_w):
    n_a = ssm_in_w.shape[0]
    n_b = attn_in_w.shape[0]
    head_order = np.concatenate([np.arange(0, SSM_HEADS, 2), np.arange(1, SSM_HEADS, 2)])
    h = x
    for i in range(n_a):
        in_w = ssm_in_w[i]
        h = _ssm_layer(
            h, p[i], _row(ssm_norm_w[i]),
            _wchunks(in_w[:, :D_INNER]),
            _wchunks(in_w[:, D_INNER:D_INNER + CONV_CH]),
            _pad_lanes(in_w[:, D_INNER + CONV_CH:][:, head_order]).astype(BF16),
            ssm_conv_w[i].astype(F32), _row(ssm_conv_b[i]),
            _pad_lanes(_row(ssm_dt_bias[i][head_order])), _pad_lanes(_row(ssm_a_log[i][head_order])),
            _row(jnp.repeat(ssm_d[i], SSM_HEAD_DIM)), _row(ssm_gnorm_w[i]),
            _wchunks(ssm_out_w[i]),
            _row(ple_norm_w[i]), _wchunks(ple_gate_w[i]), _wchunks(ple_proj_w[i]))

    inv_freq = ROPE_THETA ** (-(jnp.arange(HALF, dtype=F32) * 2.0 / HEAD_DIM))
    kperm = _halves_perm(KV_HEADS)
    qperm = _halves_perm(ATT_HEADS)
    posf = positions.astype(F32)[..., None]
    wv = kv_w[:, KV_WIDTH:].reshape(D_MODEL, KV_HEADS, 1, HEAD_DIM)
    wv = jnp.broadcast_to(wv, (D_MODEL, KV_HEADS, 2, HEAD_DIM)).reshape(D_MODEL, 2 * KV_WIDTH)
    kt, v2, cos, sin = _shared_kv(
        h, posf, _row(kv_norm_w), kv_w[:, :KV_WIDTH][:, kperm].astype(BF16), wv.astype(BF16),
        _row(jnp.tile(k_norm_w.reshape(2, HALF), (1, KV_HEADS))),
        _row(jnp.tile(inv_freq, LANES // HALF)), _segment_ones(LANES))

    seg_q = _segment_ones(ATT_WIDTH // 2)
    for j in range(n_b):
        i = n_a + j
        in_w = attn_in_w[j]
        h = _attn_layer(
            h, p[i], attn_sinks[j].astype(F32), cos, sin, kt, v2, _row(attn_norm_w[j]),
            _wchunks(in_w[:, :ATT_WIDTH][:, qperm]), _wchunks(in_w[:, ATT_WIDTH:]),
            _row(jnp.tile(q_norm_w[j].reshape(2, HALF), (1, ATT_HEADS))), seg_q,
            _wchunks(attn_out_w[j]),
            _row(ple_norm_w[i]), _wchunks(ple_gate_w[i]), _wchunks(ple_proj_w[i]))
    return h
```

```python
import jax
import jax.numpy as jnp
import numpy as np
from jax import lax
from jax.experimental import pallas as pl
from jax.experimental.pallas import tpu as pltpu

F32 = jnp.float32
BF16 = jnp.bfloat16

D_MODEL = 1024
D_INNER = 2048
SSM_HEAD_DIM = 64
SSM_HEADS = 32
SSM_GROUPS = 4
SSM_HPG = 8
D_STATE = 128
CONV_WIDTH = 4
BC_WIDTH = SSM_GROUPS * D_STATE
CONV_CH = D_INNER + 2 * BC_WIDTH
CHUNK = 128
ATT_HEADS = 16
KV_HEADS = 4
HEAD_DIM = 64
HALF = HEAD_DIM // 2
Q_PER_KV = 4
ATT_WIDTH = 1024
KV_WIDTH = 256
BLOCK = 128
ROPE_THETA = 10000.0
PLE_DIM = 256
EPS = 1e-6

LANES = 128
SUBLANES = 8
TS = 256
ATS = 512
ANB = ATS // BLOCK
STS = 512
SNCH = STS // CHUNK
MROWS = 256
VREGS_PER_CHUNK = CHUNK // SUBLANES
CTAIL = (CONV_WIDTH - 1) * SUBLANES
WCOL = 512
GW = D_INNER // SSM_GROUPS
PAIRS = SSM_HEADS // 2
PPG = SSM_HPG // 2
VMEM_LIMIT = 56 * 1024 * 1024

_NT = (((1,), (1,)), ((), ()))


def _dot(a, b):
    return jnp.dot(a, b, preferred_element_type=F32)


def _dotw(a, w_ref):
    return jnp.concatenate([_dot(a, w_ref[c]) for c in range(w_ref.shape[0])], axis=1)


def _rms(x, w):
    ms = jnp.mean(x * x, axis=-1, keepdims=True)
    return x * lax.rsqrt(ms + EPS) * w


def _sigmoid(x):
    return 0.5 * jnp.tanh(0.5 * x) + 0.5


def _silu(x):
    hx = 0.5 * x
    return hx * jnp.tanh(hx) + hx


def _split2(x):
    hi = x.astype(BF16)
    lo = (x - hi.astype(F32)).astype(BF16)
    return hi, lo


def _ple(h2, p_tile, pnw_ref, pgw_ref, ppw_ref):
    u2 = _rms(h2, pnw_ref[...]).astype(BF16)
    gate = _sigmoid(_dotw(u2, pgw_ref))
    proj = _dotw(p_tile.astype(BF16), ppw_ref)
    return h2 + gate * proj


def _time_of_row(r):
    return (r & (SUBLANES - 1)) * VREGS_PER_CHUNK + (r >> 3)


def _scatter_rows(ref, tile0, row0, val, to_kernel_order):
    for tile in range(val.shape[1] // LANES):
        for i in range(val.shape[0] // SUBLANES):
            chunk, v = divmod(i, VREGS_PER_CHUNK)
            if to_kernel_order:
                rows = pl.ds(row0 + chunk * CHUNK + (CHUNK // 2) * (v % 2) + v // 2, SUBLANES, stride=SUBLANES)
            else:
                rows = pl.ds(row0 + chunk * CHUNK + v, SUBLANES, stride=VREGS_PER_CHUNK)
            piece = val[i * SUBLANES:(i + 1) * SUBLANES, tile * LANES:(tile + 1) * LANES]
            if len(ref.shape) == 3:
                ref[pl.ds(tile0 + tile, 1), rows, :] = piece[None]
            else:
                ref[rows, :] = piece


def _gather_tiles(ref, tile0, n_tiles, rows):
    return jnp.concatenate([ref[tile0 + k, rows, :] for k in range(n_tiles)], axis=1)


def _ssm_kernel(x_ref, p_ref, nw_ref, wz_ref, wxbc_ref, wdt_ref, cw_ref, cb_ref, dtb_ref, alog_ref,
                dsk_ref, gnw_ref, ow_ref, pnw_ref, pgw_ref, ppw_ref, o_ref,
                cbuf, u_s, dt_s, zs_s, y16_s, h2_s, st_s):
    t = pl.program_id(1)

    @pl.when(t == 0)
    def _():
        cbuf[:, 0:CTAIL, :] = jnp.zeros((CONV_CH // LANES, CTAIL, LANES), F32)
        st_s[...] = jnp.zeros(st_s.shape, F32)

    @pl.when(t > 0)
    def _():
        cbuf[:, 0:CTAIL, :] = cbuf[:, STS:STS + CTAIL, :]

    u_s[...] = _rms(x_ref[0], nw_ref[...]).astype(BF16)
    _scatter_rows(dt_s, 0, 0, _dot(u_s[...], wdt_ref[...]) + dtb_ref[...], True)
    dt_raw = dt_s[...]
    dt = jnp.maximum(dt_raw, 0.0) + jnp.log1p(jnp.exp(-jnp.abs(dt_raw)))
    d_a = dt * (-jnp.exp(alog_ref[...]))

    ii = lax.broadcasted_iota(jnp.int32, (CHUNK, LANES), 0)
    ll = lax.broadcasted_iota(jnp.int32, (CHUNK, LANES), 1)
    lo_half = ll < SSM_HEAD_DIM
    time_i = _time_of_row(ii)
    time_lo = _time_of_row(jnp.where(lo_half, ll, ll - SSM_HEAD_DIM))
    mask_lo = time_lo <= time_i
    mask_hi = time_lo + SUBLANES <= time_i
    ltri = (_time_of_row(ll) <= time_i).astype(F32)
    lo_row = lax.broadcasted_iota(jnp.int32, (PAIRS, LANES), 1) < SSM_HEAD_DIM
    lo_k = lax.broadcasted_iota(jnp.int32, (SSM_HEAD_DIM, LANES), 1) < SSM_HEAD_DIM

    tabs = []
    for c in range(SNCH):
        r0 = c * CHUNK
        dt_c = dt[r0:r0 + CHUNK]
        acum = jnp.dot(ltri, d_a[r0:r0 + CHUNK], precision=lax.Precision.HIGHEST,
                       preferred_element_type=F32)
        acum_t = acum.T
        dt_t = dt_c.T
        ev, od = acum_t[:PAIRS], acum_t[PAIRS:2 * PAIRS]
        dev, dod = dt_t[:PAIRS], dt_t[PAIRS:2 * PAIRS]
        tabs.append(dict(
            acum=acum,
            w_out=jnp.exp(acum[CHUNK - 1:CHUNK, :] - acum) * dt_c,
            a_lo=jnp.where(lo_row, ev, pltpu.roll(od, SSM_HEAD_DIM, 1)),
            a_hi=jnp.where(lo_row, pltpu.roll(ev, SSM_HEAD_DIM, 1), od),
            d_lo=jnp.where(lo_row, dev, pltpu.roll(dod, SSM_HEAD_DIM, 1)),
            d_hi=jnp.where(lo_row, pltpu.roll(dev, SSM_HEAD_DIM, 1), dod)))

    def pair_expand(v, pair):
        idx = jnp.where(lo_half, pair, PAIRS + pair)
        return jnp.take_along_axis(v, idx, axis=1)

    tiles_per_wcol = WCOL // LANES

    def proj_item(hf, j):
        r0 = hf * MROWS
        _scatter_rows(cbuf, j * tiles_per_wcol, CTAIL + r0, _dot(u_s[r0:r0 + MROWS, :], wxbc_ref[j]), True)

    def z_item(hf, g):
        r0 = hf * MROWS
        _scatter_rows(zs_s, g * tiles_per_wcol, r0, _silu(_dot(u_s[r0:r0 + MROWS, :], wz_ref[g])), True)

    def conv(r0, c0, width):
        base = CTAIL + r0
        tile0, n_tiles = c0 // LANES, width // LANES
        x = _gather_tiles(cbuf, tile0, n_tiles, slice(base, base + CHUNK))
        last = lax.broadcasted_iota(jnp.int32, (SUBLANES, width), 0) == SUBLANES - 1
        wrapped = []
        for v in range(VREGS_PER_CHUNK - CONV_WIDTH + 1, VREGS_PER_CHUNK):
            prev = _gather_tiles(cbuf, tile0, n_tiles,
                                 slice(base - CHUNK + v * SUBLANES, base - CHUNK + (v + 1) * SUBLANES))
            wrapped.append(pltpu.roll(jnp.where(last, prev, x[v * SUBLANES:(v + 1) * SUBLANES]), 1, 0))
        acc = cb_ref[:, c0:c0 + width] + cw_ref[CONV_WIDTH - 1:CONV_WIDTH, c0:c0 + width] * x
        for k in range(1, CONV_WIDTH):
            shifted = jnp.concatenate(wrapped[CONV_WIDTH - 1 - k:] + [x[:CHUNK - k * SUBLANES]], axis=0)
            acc = acc + cw_ref[CONV_WIDTH - 1 - k:CONV_WIDTH - k, c0:c0 + width] * shifted
        return _silu(acc)

    def ssd_item(c, g):
        r0, gc, tab = c * CHUNK, g * GW, tabs[c]
        xs = conv(r0, gc, GW)
        bm = conv(r0, D_INNER + g * D_STATE, D_STATE)
        cm = conv(r0, D_INNER + BC_WIDTH + g * D_STATE, D_STATE)
        cm16, bm16 = cm.astype(BF16), bm.astype(BF16)
        h0, h1 = bm16[:SSM_HEAD_DIM], bm16[SSM_HEAD_DIM:]
        cb2 = lax.dot_general(cm16, jnp.concatenate([h0, h0, h1, h1], axis=0), _NT,
                              preferred_element_type=F32)
        cb_lo, cb_hi = cb2[:, :LANES], cb2[:, LANES:]
        a_exp = [pair_expand(tab["acum"], g * PPG + q) for q in range(PPG)]
        w_exp = jnp.concatenate([pair_expand(tab["w_out"], g * PPG + q) for q in range(PPG)], axis=1)
        a_all = jnp.concatenate(a_exp, axis=1)
        st = st_s[g]
        y_off = _dot(cm16, st.astype(BF16))
        st_s[g] = st * jnp.exp(a_all[CHUNK - 1:CHUNK, :]) + _dot(bm.T.astype(BF16), (xs * w_exp).astype(BF16))

        y_diag = []
        for q in range(PPG):
            pair = g * PPG + q
            t_lo = cb_lo * jnp.exp(jnp.where(mask_lo, a_exp[q] - tab["a_lo"][pair:pair + 1, :], -jnp.inf))
            t_hi = cb_hi * jnp.exp(jnp.where(mask_hi, a_exp[q] - tab["a_hi"][pair:pair + 1, :], -jnp.inf))
            lhs = jnp.concatenate([(t_lo * tab["d_lo"][pair:pair + 1, :]).astype(BF16),
                                   (t_hi * tab["d_hi"][pair:pair + 1, :]).astype(BF16)], axis=1)
            xs16 = xs[:, q * LANES:(q + 1) * LANES].astype(BF16)
            x0, x1 = xs16[:SSM_HEAD_DIM], xs16[SSM_HEAD_DIM:]
            zero = jnp.zeros_like(x0)
            rhs = jnp.concatenate([jnp.where(lo_k, x0, zero), jnp.where(lo_k, zero, x0),
                                   jnp.where(lo_k, x1, zero), jnp.where(lo_k, zero, x1)], axis=0)
            y_diag.append(_dot(lhs, rhs))
        y = jnp.concatenate(y_diag, axis=1) + jnp.exp(a_all) * y_off + dsk_ref[:, gc:gc + GW] * xs
        y = y * _gather_tiles(zs_s, gc // LANES, GW // LANES, slice(r0, r0 + CHUNK))
        y = y * lax.rsqrt(jnp.mean(y * y, axis=-1, keepdims=True) + EPS) * gnw_ref[:, gc:gc + GW]
        y16_s[r0:r0 + CHUNK, gc:gc + GW] = y.astype(BF16)

    def out_item(hf, k):
        r0 = hf * MROWS
        _scatter_rows(h2_s, k * tiles_per_wcol, r0, _dot(y16_s[r0:r0 + MROWS, :], ow_ref[k]), False)

    def ple_item(hf):
        r0 = hf * MROWS
        h2 = x_ref[0, r0:r0 + MROWS, :] + _gather_tiles(h2_s, 0, D_MODEL // LANES, slice(r0, r0 + MROWS))
        o_ref[0, r0:r0 + MROWS, :] = _ple(h2, p_ref[0, r0:r0 + MROWS, :], pnw_ref, pgw_ref, ppw_ref)

    p_, z_, s_, o_, g_ = proj_item, z_item, ssd_item, out_item, ple_item
    bc = (D_INNER // WCOL, D_INNER // WCOL + 1)
    order = [(p_, 0, bc[0]), (p_, 0, bc[1]), (p_, 0, 0), (z_, 0, 0), (p_, 0, 1),
             (s_, 0, 0), (p_, 0, 2), (z_, 0, 1),
             (s_, 0, 1), (p_, 0, 3), (z_, 0, 2),
             (s_, 0, 2), (z_, 0, 3), (p_, 1, bc[0]),
             (s_, 0, 3), (p_, 1, bc[1]), (p_, 1, 0),
             (s_, 1, 0), (z_, 1, 0), (p_, 1, 1),
             (s_, 1, 1), (p_, 1, 2), (z_, 1, 1),
             (s_, 1, 2), (p_, 1, 3), (z_, 1, 2),
             (s_, 1, 3), (z_, 1, 3),
             (s_, 2, 0), (o_, 0, 0),
             (s_, 2, 1), (s_, 2, 2), (o_, 0, 1),
             (s_, 2, 3), (s_, 3, 0), (g_, 0),
             (s_, 3, 1), (s_, 3, 2), (s_, 3, 3),
             (o_, 1, 0), (o_, 1, 1), (g_, 1)]
    assert SNCH == 4 and STS == 2 * MROWS
    for item, *args in order:
        item(*args)


def _const_spec(shape):
    nd = len(shape)
    return pl.BlockSpec(shape, lambda b, t: (0,) * nd, pipeline_mode=pl.Buffered(1))


def _ssm_layer(h, p_i, nw, wz, wxbc, wdt, cw, cb, dtb, alog, dsk, gnw, ow, pnw, pgw, ppw):
    bsz, seq, _ = h.shape
    weights = (nw, wz, wxbc, wdt, cw, cb, dtb, alog, dsk, gnw, ow, pnw, pgw, ppw)
    return pl.pallas_call(
        _ssm_kernel,
        grid=(bsz, seq // STS),
        in_specs=[pl.BlockSpec((1, STS, D_MODEL), lambda b, t: (b, t, 0)),
                  pl.BlockSpec((1, STS, PLE_DIM), lambda b, t: (b, t, 0))]
                 + [_const_spec(w.shape) for w in weights],
        out_specs=pl.BlockSpec((1, STS, D_MODEL), lambda b, t: (b, t, 0)),
        out_shape=jax.ShapeDtypeStruct(h.shape, F32),
        scratch_shapes=[pltpu.VMEM((CONV_CH // LANES, CTAIL + STS, LANES), F32),
                        pltpu.VMEM((STS, D_MODEL), BF16),
                        pltpu.VMEM((STS, LANES), F32),
                        pltpu.VMEM((D_INNER // LANES, STS, LANES), F32),
                        pltpu.VMEM((STS, D_INNER), BF16),
                        pltpu.VMEM((D_MODEL // LANES, STS, LANES), F32),
                        pltpu.VMEM((SSM_GROUPS, D_STATE, GW), F32)],
        compiler_params=pltpu.CompilerParams(dimension_semantics=("arbitrary", "arbitrary"),
                                             vmem_limit_bytes=VMEM_LIMIT),
        name="ssm_layer",
    )(h, p_i, *weights)


def _kv_kernel(x_ref, pos_ref, nw_ref, wk_ref, wv_ref, knw_ref, invf_ref, seg_ref,
               kt_ref, v_ref, cos_ref, sin_ref):
    u = _rms(x_ref[0], nw_ref[...]).astype(BF16)
    ang = pos_ref[0] * invf_ref[...]
    cos = jnp.cos(ang)
    sin = jnp.sin(ang)
    cos_ref[0] = cos
    sin_ref[0] = sin
    kk = _dot(u, wk_ref[...])
    k1, k2 = kk[:, :LANES], kk[:, LANES:]
    zhi, zlo = _split2(k1 * k1 + k2 * k2)
    ssum = _dot(zhi, seg_ref[...]) + _dot(zlo, seg_ref[...])
    rs = lax.rsqrt(ssum * (1.0 / HEAD_DIM) + EPS)
    n1 = k1 * rs * knw_ref[:, :LANES]
    n2 = k2 * rs * knw_ref[:, LANES:]
    kr = jnp.concatenate([n1 * cos - n2 * sin, n2 * cos + n1 * sin], axis=1)
    kt_ref[0] = kr.T.astype(BF16)
    v_ref[0] = _dot(u, wv_ref[...]).astype(BF16)


def _shared_kv(h, posf, nw, wk, wv, knw, invf, seg):
    bsz, seq, _ = h.shape
    consts = (nw, wk, wv, knw, invf, seg)
    return pl.pallas_call(
        _kv_kernel,
        grid=(bsz, seq // TS),
        in_specs=[pl.BlockSpec((1, TS, D_MODEL), lambda b, t: (b, t, 0)),
                  pl.BlockSpec((1, TS, 1), lambda b, t: (b, t, 0))]
                 + [_const_spec(w.shape) for w in consts],
        out_specs=[pl.BlockSpec((1, KV_WIDTH, TS), lambda b, t: (b, 0, t)),
                   pl.BlockSpec((1, TS, 2 * KV_WIDTH), lambda b, t: (b, t, 0)),
                   pl.BlockSpec((1, TS, LANES), lambda b, t: (b, t, 0)),
                   pl.BlockSpec((1, TS, LANES), lambda b, t: (b, t, 0))],
        out_shape=[jax.ShapeDtypeStruct((bsz, KV_WIDTH, seq), BF16),
                   jax.ShapeDtypeStruct((bsz, seq, 2 * KV_WIDTH), BF16),
                   jax.ShapeDtypeStruct((bsz, seq, LANES), F32),
                   jax.ShapeDtypeStruct((bsz, seq, LANES), F32)],
        compiler_params=pltpu.CompilerParams(dimension_semantics=("arbitrary", "arbitrary"),
                                             vmem_limit_bytes=VMEM_LIMIT),
        name="shared_kv",
    )(h, posf, *consts)


def _attn_kernel(sink_ref, x_ref, p_ref, cos_ref, sin_ref, ktc_ref, ktp_ref, vc_ref, vp_ref,
                 nw_ref, wq_ref, wg_ref, qnw_ref, seg_ref, ow_ref, pnw_ref, pgw_ref, ppw_ref, o_ref,
                 u_s, q16_s, krhs_s, gs_s, o_s, h2_s):
    t = pl.program_id(1)
    qw = ATT_WIDTH // 2

    @pl.when(t == 0)
    def _():
        krhs_s[...] = jnp.zeros(krhs_s.shape, BF16)

    u_s[...] = _rms(x_ref[0], nw_ref[...]).astype(BF16)

    ii = lax.broadcasted_iota(jnp.int32, (BLOCK, BLOCK), 0)
    jj = lax.broadcasted_iota(jnp.int32, (BLOCK, BLOCK), 1)
    own_ok = jj <= ii
    first_off = jnp.where(t > 0, 0, BLOCK)
    valid_first = jnp.concatenate([jj > ii + first_off, own_ok], axis=1)
    valid_rest = jnp.concatenate([jj > ii, own_ok], axis=1)
    lo_half = lax.broadcasted_iota(jnp.int32, (2 * BLOCK, LANES), 1) < HEAD_DIM
    scale = HEAD_DIM ** -0.5

    def q_item(b):
        r0 = b * BLOCK
        u = u_s[r0:r0 + BLOCK, :]
        q1 = _dot(u, wq_ref[0])
        q2 = _dot(u, wq_ref[1])
        zhi, zlo = _split2(q1 * q1 + q2 * q2)
        ssum = _dot(zhi, seg_ref[...]) + _dot(zlo, seg_ref[...])
        rs = lax.rsqrt(ssum * (1.0 / HEAD_DIM) + EPS)
        n1 = q1 * rs * qnw_ref[:, :qw]
        n2 = q2 * rs * qnw_ref[:, qw:]
        cos = jnp.concatenate([cos_ref[0, r0:r0 + BLOCK, :]] * (qw // LANES), axis=1)
        sin = jnp.concatenate([sin_ref[0, r0:r0 + BLOCK, :]] * (qw // LANES), axis=1)
        q16_s[r0:r0 + BLOCK, :qw] = ((n1 * cos - n2 * sin) * scale).astype(BF16)
        q16_s[r0:r0 + BLOCK, qw:] = ((n2 * cos + n1 * sin) * scale).astype(BF16)

    def gate_item(b, k):
        r0 = b * BLOCK
        gs_s[r0:r0 + BLOCK, k * WCOL:(k + 1) * WCOL] = _silu(_dot(u_s[r0:r0 + BLOCK, :], wg_ref[k]))

    def kv_blocks(b):
        r0 = b * BLOCK
        if b == 0:
            return ktp_ref[0], ktc_ref[0, :, 0:BLOCK], vp_ref[0], vc_ref[0, 0:BLOCK, :]
        return (ktc_ref[0, :, r0 - BLOCK:r0], ktc_ref[0, :, r0:r0 + BLOCK],
                vc_ref[0, r0 - BLOCK:r0, :], vc_ref[0, r0:r0 + BLOCK, :])

    scores = {}

    def score_item(b, g):
        r0 = b * BLOCK
        slot = (b * KV_HEADS + g) % 2
        kprev, kown, _, _ = kv_blocks(b)
        for hl in range(Q_PER_KV):
            for half in range(2):
                src = half * LANES + g * HALF
                dst = half * LANES + hl * HALF
                krhs_s[slot, dst:dst + HALF, hl * 2 * BLOCK:hl * 2 * BLOCK + BLOCK] = kprev[src:src + HALF, :]
                krhs_s[slot, dst:dst + HALF, hl * 2 * BLOCK + BLOCK:(hl + 1) * 2 * BLOCK] = kown[src:src + HALF, :]
        lhs = jnp.concatenate([q16_s[r0:r0 + BLOCK, g * LANES:(g + 1) * LANES],
                               q16_s[r0:r0 + BLOCK, qw + g * LANES:qw + (g + 1) * LANES]], axis=1)
        scores[(b, g)] = _dot(lhs, krhs_s[slot])

    def softmax_pv_item(b, g):
        r0 = b * BLOCK
        s_all = scores.pop((b, g))
        valid = valid_first if b == 0 else valid_rest
        _, _, vprev, vown = kv_blocks(b)
        vg = jnp.concatenate([vprev[:, g * LANES:(g + 1) * LANES], vown[:, g * LANES:(g + 1) * LANES]], axis=0)
        zero = jnp.zeros_like(vg)
        vbd = jnp.concatenate([jnp.where(lo_half, vg, zero), jnp.where(lo_half, zero, vg)], axis=0)
        probs = []
        for hl in range(Q_PER_KV):
            sink = sink_ref[g * Q_PER_KV + hl]
            s = jnp.where(valid, s_all[:, hl * 2 * BLOCK:(hl + 1) * 2 * BLOCK], -jnp.inf)
            m = jnp.maximum(jnp.max(s, axis=-1, keepdims=True), sink)
            e = jnp.exp(s - m)
            den = jnp.sum(e, axis=-1, keepdims=True) + jnp.exp(sink - m)
            probs.append((e * (1.0 / den)).astype(BF16))
        for pr in range(Q_PER_KV // 2):
            lhs_p = jnp.concatenate([probs[2 * pr], probs[2 * pr + 1]], axis=1)
            c0 = (g * 2 + pr) * LANES
            o_s[r0:r0 + BLOCK, c0:c0 + LANES] = _dot(lhs_p, vbd)

    def out_item(b):
        r0 = b * BLOCK
        og = (o_s[r0:r0 + BLOCK, :] * gs_s[r0:r0 + BLOCK, :]).astype(BF16)
        h2_s[r0:r0 + BLOCK, :] = x_ref[0, r0:r0 + BLOCK, :] + _dotw(og, ow_ref)

    def ple_item(b):
        r0 = b * BLOCK
        o_ref[0, r0:r0 + BLOCK, :] = _ple(h2_s[r0:r0 + BLOCK, :], p_ref[0, r0:r0 + BLOCK, :],
                                          pnw_ref, pgw_ref, ppw_ref)

    q_item(0)
    score_item(0, 0)
    for b in range(ANB):
        more = b + 1 < ANB
        for g in range(KV_HEADS):
            if g + 1 < KV_HEADS:
                score_item(b, g + 1)
            elif more:
                score_item(b + 1, 0)
            if g == 0:
                gate_item(b, 0)
                gate_item(b, 1)
            elif g == 1 and b > 0:
                out_item(b - 1)
            elif g == 2 and more:
                q_item(b + 1)
            elif g == 3 and b > 0:
                ple_item(b - 1)
            softmax_pv_item(b, g)
    out_item(ANB - 1)
    ple_item(ANB - 1)


def _attn_layer(h, p_i, sinks, cos, sin, kt, v2, nw, wq, wg, qnw, seg, ow, pnw, pgw, ppw):
    bsz, seq, _ = h.shape
    consts = (nw, wq, wg, qnw, seg, ow, pnw, pgw, ppw)
    prev_blk = lambda t: jnp.maximum(t * ANB - 1, 0)
    return pl.pallas_call(
        _attn_kernel,
        grid=(bsz, seq // ATS),
        in_specs=[pl.BlockSpec(memory_space=pltpu.SMEM),
                  pl.BlockSpec((1, ATS, D_MODEL), lambda b, t: (b, t, 0)),
                  pl.BlockSpec((1, ATS, PLE_DIM), lambda b, t: (b, t, 0)),
                  pl.BlockSpec((1, ATS, LANES), lambda b, t: (b, t, 0)),
                  pl.BlockSpec((1, ATS, LANES), lambda b, t: (b, t, 0)),
                  pl.BlockSpec((1, KV_WIDTH, ATS), lambda b, t: (b, 0, t)),
                  pl.BlockSpec((1, KV_WIDTH, BLOCK), lambda b, t: (b, 0, prev_blk(t))),
                  pl.BlockSpec((1, ATS, 2 * KV_WIDTH), lambda b, t: (b, t, 0)),
                  pl.BlockSpec((1, BLOCK, 2 * KV_WIDTH), lambda b, t: (b, prev_blk(t), 0))]
                 + [_const_spec(w.shape) for w in consts],
        out_specs=pl.BlockSpec((1, ATS, D_MODEL), lambda b, t: (b, t, 0)),
        out_shape=jax.ShapeDtypeStruct(h.shape, F32),
        scratch_shapes=[pltpu.VMEM((ATS, D_MODEL), BF16),
                        pltpu.VMEM((ATS, ATT_WIDTH), BF16),
                        pltpu.VMEM((2, 2 * LANES, Q_PER_KV * 2 * BLOCK), BF16),
                        pltpu.VMEM((ATS, ATT_WIDTH), F32),
                        pltpu.VMEM((ATS, ATT_WIDTH), F32),
                        pltpu.VMEM((ATS, D_MODEL), F32)],
        compiler_params=pltpu.CompilerParams(dimension_semantics=("arbitrary", "arbitrary"),
                                             vmem_limit_bytes=VMEM_LIMIT),
        name="attn_layer",
    )(sinks, h, p_i, cos, sin, kt, kt, v2, v2, *consts)


def _row(v):
    return v.reshape(1, -1).astype(F32)


def _pad_lanes(v):
    return jnp.pad(v, [(0, 0)] * (v.ndim - 1) + [(0, LANES - v.shape[-1])])


def _wchunks(w):
    k, n = w.shape
    return w.astype(BF16).reshape(k, n // WCOL, WCOL).transpose(1, 0, 2)


def _halves_perm(n_heads):
    d = np.arange(HALF)
    first = (np.arange(n_heads)[:, None] * HEAD_DIM + d[None, :]).reshape(-1)
    return np.concatenate([first, first + HALF])


def _segment_ones(width):
    idx = np.arange(width) // HALF
    return jnp.asarray(idx[:, None] == idx[None, :], dtype=BF16)


def kernel(x, p, positions, ssm_norm_w, ssm_in_w, ssm_conv_w, ssm_conv_b, ssm_dt_bias, ssm_a_log, ssm_d,
           ssm_gnorm_w, ssm_out_w, kv_norm_w, kv_w, k_norm_w, attn_norm_w, attn_in_w, q_norm_w, attn_sinks,
           attn_out_w, ple_norm_w, ple_gate_w, ple_proj_w):
    n_a = ssm_in_w.shape[0]
    n_b = attn_in_w.shape[0]
    head_order = np.concatenate([np.arange(0, SSM_HEADS, 2), np.arange(1, SSM_HEADS, 2)])
    h = x
    for i in range(n_a):
        in_w = ssm_in_w[i]
        h = _ssm_layer(
            h, p[i], _row(ssm_norm_w[i]),
            _wchunks(in_w[:, :D_INNER]),
            _wchunks(in_w[:, D_INNER:D_INNER + CONV_CH]),
            _pad_lanes(in_w[:, D_INNER + CONV_CH:][:, head_order]).astype(BF16),
            ssm_conv_w[i].astype(F32), _row(ssm_conv_b[i]),
            _pad_lanes(_row(ssm_dt_bias[i][head_order])), _pad_lanes(_row(ssm_a_log[i][head_order])),
            _row(jnp.repeat(ssm_d[i], SSM_HEAD_DIM)), _row(ssm_gnorm_w[i]),
            _wchunks(ssm_out_w[i]),
            _row(ple_norm_w[i]), _wchunks(ple_gate_w[i]), _wchunks(ple_proj_w[i]))

    inv_freq = ROPE_THETA ** (-(jnp.arange(HALF, dtype=F32) * 2.0 / HEAD_DIM))
    kperm = _halves_perm(KV_HEADS)
    qperm = _halves_perm(ATT_HEADS)
    posf = positions.astype(F32)[..., None]
    wv = kv_w[:, KV_WIDTH:].reshape(D_MODEL, KV_HEADS, 1, HEAD_DIM)
    wv = jnp.broadcast_to(wv, (D_MODEL, KV_HEADS, 2, HEAD_DIM)).reshape(D_MODEL, 2 * KV_WIDTH)
    kt, v2, cos, sin = _shared_kv(
        h, posf, _row(kv_norm_w), kv_w[:, :KV_WIDTH][:, kperm].astype(BF16), wv.astype(BF16),
        _row(jnp.tile(k_norm_w.reshape(2, HALF), (1, KV_HEADS))),
        _row(jnp.tile(inv_freq, LANES // HALF)), _segment_ones(LANES))

    seg_q = _segment_ones(ATT_WIDTH // 2)
    for j in range(n_b):
        i = n_a + j
        in_w = attn_in_w[j]
        h = _attn_layer(
            h, p[i], attn_sinks[j].astype(F32), cos, sin, kt, v2, _row(attn_norm_w[j]),
            _wchunks(in_w[:, :ATT_WIDTH][:, qperm]), _wchunks(in_w[:, ATT_WIDTH:]),
            _row(jnp.tile(q_norm_w[j].reshape(2, HALF), (1, ATT_HEADS))), seg_q,
            _wchunks(attn_out_w[j]),
            _row(ple_norm_w[i]), _wchunks(ple_gate_w[i]), _wchunks(ple_proj_w[i]))
    return h
```

```python
import jax
import jax.numpy as jnp
import numpy as np
from jax import lax
from jax.experimental import pallas as pl
from jax.experimental.pallas import tpu as pltpu

F32 = jnp.float32
BF16 = jnp.bfloat16

D_MODEL = 1024
D_INNER = 2048
SSM_HEAD_DIM = 64
SSM_HEADS = 32
SSM_GROUPS = 4
SSM_HPG = 8
D_STATE = 128
CONV_WIDTH = 4
BC_WIDTH = SSM_GROUPS * D_STATE
CONV_CH = D_INNER + 2 * BC_WIDTH
CHUNK = 128
ATT_HEADS = 16
KV_HEADS = 4
HEAD_DIM = 64
HALF = HEAD_DIM // 2
Q_PER_KV = 4
ATT_WIDTH = 1024
KV_WIDTH = 256
BLOCK = 128
ROPE_THETA = 10000.0
PLE_DIM = 256
EPS = 1e-6

LANES = 128
SUBLANES = 8
TS = 256
ATS = 512
ANB = ATS // BLOCK
STS = 512
SNCH = STS // CHUNK
MROWS = 256
VREGS_PER_CHUNK = CHUNK // SUBLANES
CTAIL = (CONV_WIDTH - 1) * SUBLANES
WCOL = 512
GW = D_INNER // SSM_GROUPS
PAIRS = SSM_HEADS // 2
PPG = SSM_HPG // 2
VMEM_LIMIT = 56 * 1024 * 1024

_NT = (((1,), (1,)), ((), ()))


def _dot(a, b):
    return jnp.dot(a, b, preferred_element_type=F32)


def _dotw(a, w_refs):
    return jnp.concatenate([_dot(a, w[...]) for w in w_refs], axis=1)


def _rms(x, w):
    ms = jnp.mean(x * x, axis=-1, keepdims=True)
    return x * lax.rsqrt(ms + EPS) * w


def _sigmoid(x):
    return 0.5 * jnp.tanh(0.5 * x) + 0.5


def _silu(x):
    hx = 0.5 * x
    return hx * jnp.tanh(hx) + hx


def _split2(x):
    hi = x.astype(BF16)
    lo = (x - hi.astype(F32)).astype(BF16)
    return hi, lo


def _ple(h2, p_tile, pnw_ref, pgw_ref, ppw_ref):
    u2 = _rms(h2, pnw_ref[...]).astype(BF16)
    gate = _sigmoid(_dotw(u2, pgw_ref))
    proj = _dotw(p_tile.astype(BF16), ppw_ref)
    return h2 + gate * proj


def _time_of_row(r):
    return (r & (SUBLANES - 1)) * VREGS_PER_CHUNK + (r >> 3)


def _scatter_rows(ref, tile0, row0, val, to_kernel_order):
    for tile in range(val.shape[1] // LANES):
        for i in range(val.shape[0] // SUBLANES):
            chunk, v = divmod(i, VREGS_PER_CHUNK)
            if to_kernel_order:
                rows = pl.ds(row0 + chunk * CHUNK + (CHUNK // 2) * (v % 2) + v // 2, SUBLANES, stride=SUBLANES)
            else:
                rows = pl.ds(row0 + chunk * CHUNK + v, SUBLANES, stride=VREGS_PER_CHUNK)
            piece = val[i * SUBLANES:(i + 1) * SUBLANES, tile * LANES:(tile + 1) * LANES]
            if len(ref.shape) == 3:
                ref[pl.ds(tile0 + tile, 1), rows, :] = piece[None]
            else:
                ref[rows, :] = piece


def _gather_tiles(ref, tile0, n_tiles, rows):
    return jnp.concatenate([ref[tile0 + k, rows, :] for k in range(n_tiles)], axis=1)


def _take(refs, counts):
    out, i = [], 0
    for n in counts:
        out.append(refs[i] if n is None else refs[i:i + n])
        i += 1 if n is None else n
    assert i == len(refs)
    return out


_SSM_REF_COUNTS = (None, None, None, D_INNER // WCOL, CONV_CH // WCOL, None, None, None, None, None, None, None,
                   D_MODEL // WCOL, None, D_MODEL // WCOL, D_MODEL // WCOL, None, None, None, None, None, None, None, None)


def _ssm_kernel(*refs):
    (x_ref, p_ref, nw_ref, wz_ref, wxbc_ref, wdt_ref, cw_ref, cb_ref, dtb_ref, alog_ref, dsk_ref, gnw_ref,
     ow_ref, pnw_ref, pgw_ref, ppw_ref, o_ref, cbuf, u_s, dt_s, zs_s, y16_s, h2_s, st_s) = _take(refs, _SSM_REF_COUNTS)
    t = pl.program_id(1)

    @pl.when(t == 0)
    def _():
        cbuf[:, 0:CTAIL, :] = jnp.zeros((CONV_CH // LANES, CTAIL, LANES), F32)
        st_s[...] = jnp.zeros(st_s.shape, F32)

    @pl.when(t > 0)
    def _():
        cbuf[:, 0:CTAIL, :] = cbuf[:, STS:STS + CTAIL, :]

    u_s[...] = _rms(x_ref[0], nw_ref[...]).astype(BF16)
    _scatter_rows(dt_s, 0, 0, _dot(u_s[...], wdt_ref[...]) + dtb_ref[...], True)
    dt_raw = dt_s[...]
    dt = jnp.maximum(dt_raw, 0.0) + jnp.log1p(jnp.exp(-jnp.abs(dt_raw)))
    d_a = dt * (-jnp.exp(alog_ref[...]))

    ii = lax.broadcasted_iota(jnp.int32, (CHUNK, LANES), 0)
    ll = lax.broadcasted_iota(jnp.int32, (CHUNK, LANES), 1)
    lo_half = ll < SSM_HEAD_DIM
    time_i = _time_of_row(ii)
    time_lo = _time_of_row(jnp.where(lo_half, ll, ll - SSM_HEAD_DIM))
    mask_lo = time_lo <= time_i
    mask_hi = time_lo + SUBLANES <= time_i
    ltri = (_time_of_row(ll) <= time_i).astype(F32)
    lo_row = lax.broadcasted_iota(jnp.int32, (PAIRS, LANES), 1) < SSM_HEAD_DIM
    lo_k = lax.broadcasted_iota(jnp.int32, (SSM_HEAD_DIM, LANES), 1) < SSM_HEAD_DIM

    tabs = []
    for c in range(SNCH):
        r0 = c * CHUNK
        dt_c = dt[r0:r0 + CHUNK]
        acum = jnp.dot(ltri, d_a[r0:r0 + CHUNK], precision=lax.Precision.HIGHEST,
                       preferred_element_type=F32)
        acum_t = acum.T
        dt_t = dt_c.T
        ev, od = acum_t[:PAIRS], acum_t[PAIRS:2 * PAIRS]
        dev, dod = dt_t[:PAIRS], dt_t[PAIRS:2 * PAIRS]
        tabs.append(dict(
            acum=acum,
            w_out=jnp.exp(acum[CHUNK - 1:CHUNK, :] - acum) * dt_c,
            a_lo=jnp.where(lo_row, ev, pltpu.roll(od, SSM_HEAD_DIM, 1)),
            a_hi=jnp.where(lo_row, pltpu.roll(ev, SSM_HEAD_DIM, 1), od),
            d_lo=jnp.where(lo_row, dev, pltpu.roll(dod, SSM_HEAD_DIM, 1)),
            d_hi=jnp.where(lo_row, pltpu.roll(dev, SSM_HEAD_DIM, 1), dod)))

    def pair_expand(v, pair):
        idx = jnp.where(lo_half, pair, PAIRS + pair)
        return jnp.take_along_axis(v, idx, axis=1)

    tiles_per_wcol = WCOL // LANES

    def proj_item(hf, j):
        r0 = hf * MROWS
        _scatter_rows(cbuf, j * tiles_per_wcol, CTAIL + r0, _dot(u_s[r0:r0 + MROWS, :], wxbc_ref[j][...]), True)

    def z_item(hf, g):
        r0 = hf * MROWS
        _scatter_rows(zs_s, g * tiles_per_wcol, r0, _silu(_dot(u_s[r0:r0 + MROWS, :], wz_ref[g][...])), True)

    def conv(r0, c0, width):
        base = CTAIL + r0
        tile0, n_tiles = c0 // LANES, width // LANES
        x = _gather_tiles(cbuf, tile0, n_tiles, slice(base, base + CHUNK))
        last = lax.broadcasted_iota(jnp.int32, (SUBLANES, width), 0) == SUBLANES - 1
        wrapped = []
        for v in range(VREGS_PER_CHUNK - CONV_WIDTH + 1, VREGS_PER_CHUNK):
            prev = _gather_tiles(cbuf, tile0, n_tiles,
                                 slice(base - CHUNK + v * SUBLANES, base - CHUNK + (v + 1) * SUBLANES))
            wrapped.append(pltpu.roll(jnp.where(last, prev, x[v * SUBLANES:(v + 1) * SUBLANES]), 1, 0))
        acc = cb_ref[:, c0:c0 + width] + cw_ref[CONV_WIDTH - 1:CONV_WIDTH, c0:c0 + width] * x
        for k in range(1, CONV_WIDTH):
            shifted = jnp.concatenate(wrapped[CONV_WIDTH - 1 - k:] + [x[:CHUNK - k * SUBLANES]], axis=0)
            acc = acc + cw_ref[CONV_WIDTH - 1 - k:CONV_WIDTH - k, c0:c0 + width] * shifted
        return _silu(acc)

    def ssd_item(c, g):
        r0, gc, tab = c * CHUNK, g * GW, tabs[c]
        xs = conv(r0, gc, GW)
        bm = conv(r0, D_INNER + g * D_STATE, D_STATE)
        cm = conv(r0, D_INNER + BC_WIDTH + g * D_STATE, D_STATE)
        cm16, bm16 = cm.astype(BF16), bm.astype(BF16)
        h0, h1 = bm16[:SSM_HEAD_DIM], bm16[SSM_HEAD_DIM:]
        cb2 = lax.dot_general(cm16, jnp.concatenate([h0, h0, h1, h1], axis=0), _NT,
                              preferred_element_type=F32)
        cb_lo, cb_hi = cb2[:, :LANES], cb2[:, LANES:]
        a_exp = [pair_expand(tab["acum"], g * PPG + q) for q in range(PPG)]
        w_exp = jnp.concatenate([pair_expand(tab["w_out"], g * PPG + q) for q in range(PPG)], axis=1)
        a_all = jnp.concatenate(a_exp, axis=1)
        st = st_s[g]
        y_off = _dot(cm16, st.astype(BF16))
        st_s[g] = st * jnp.exp(a_all[CHUNK - 1:CHUNK, :]) + _dot(bm.T.astype(BF16), (xs * w_exp).astype(BF16))

        y_diag = []
        for q in range(PPG):
            pair = g * PPG + q
            t_lo = cb_lo * jnp.exp(jnp.where(mask_lo, a_exp[q] - tab["a_lo"][pair:pair + 1, :], -jnp.inf))
            t_hi = cb_hi * jnp.exp(jnp.where(mask_hi, a_exp[q] - tab["a_hi"][pair:pair + 1, :], -jnp.inf))
            lhs = jnp.concatenate([(t_lo * tab["d_lo"][pair:pair + 1, :]).astype(BF16),
                                   (t_hi * tab["d_hi"][pair:pair + 1, :]).astype(BF16)], axis=1)
            xs16 = xs[:, q * LANES:(q + 1) * LANES].astype(BF16)
            x0, x1 = xs16[:SSM_HEAD_DIM], xs16[SSM_HEAD_DIM:]
            zero = jnp.zeros_like(x0)
            rhs = jnp.concatenate([jnp.where(lo_k, x0, zero), jnp.where(lo_k, zero, x0),
                                   jnp.where(lo_k, x1, zero), jnp.where(lo_k, zero, x1)], axis=0)
            y_diag.append(_dot(lhs, rhs))
        y = jnp.concatenate(y_diag, axis=1) + jnp.exp(a_all) * y_off + dsk_ref[:, gc:gc + GW] * xs
        y = y * _gather_tiles(zs_s, gc // LANES, GW // LANES, slice(r0, r0 + CHUNK))
        y = y * lax.rsqrt(jnp.mean(y * y, axis=-1, keepdims=True) + EPS) * gnw_ref[:, gc:gc + GW]
        y16_s[r0:r0 + CHUNK, gc:gc + GW] = y.astype(BF16)

    def out_item(hf, k):
        r0 = hf * MROWS
        _scatter_rows(h2_s, k * tiles_per_wcol, r0, _dot(y16_s[r0:r0 + MROWS, :], ow_ref[k][...]), False)

    def ple_item(hf):
        r0 = hf * MROWS
        h2 = x_ref[0, r0:r0 + MROWS, :] + _gather_tiles(h2_s, 0, D_MODEL // LANES, slice(r0, r0 + MROWS))
        o_ref[0, r0:r0 + MROWS, :] = _ple(h2, p_ref[0, r0:r0 + MROWS, :], pnw_ref, pgw_ref, ppw_ref)

    p_, z_, s_, o_, g_ = proj_item, z_item, ssd_item, out_item, ple_item
    bc = (D_INNER // WCOL, D_INNER // WCOL + 1)
    order = [(p_, 0, bc[0]), (p_, 0, bc[1]), (p_, 0, 0), (z_, 0, 0), (p_, 0, 1),
             (s_, 0, 0), (p_, 0, 2), (z_, 0, 1),
             (s_, 0, 1), (p_, 0, 3), (z_, 0, 2),
             (s_, 0, 2), (z_, 0, 3), (p_, 1, bc[0]),
             (s_, 0, 3), (p_, 1, bc[1]), (p_, 1, 0),
             (s_, 1, 0), (z_, 1, 0), (p_, 1, 1),
             (s_, 1, 1), (p_, 1, 2), (z_, 1, 1),
             (s_, 1, 2), (p_, 1, 3), (z_, 1, 2),
             (s_, 1, 3), (z_, 1, 3),
             (s_, 2, 0), (o_, 0, 0),
             (s_, 2, 1), (s_, 2, 2), (o_, 0, 1),
             (s_, 2, 3), (s_, 3, 0), (g_, 0),
             (s_, 3, 1), (s_, 3, 2), (s_, 3, 3),
             (o_, 1, 0), (o_, 1, 1), (g_, 1)]
    assert SNCH == 4 and STS == 2 * MROWS
    for item, *args in order:
        item(*args)


def _const_spec(shape):
    nd = len(shape)
    return pl.BlockSpec(shape, lambda b, t: (0,) * nd, pipeline_mode=pl.Buffered(1))


def _chunk_spec(layer, col_block, rows):
    return pl.BlockSpec((None, rows, WCOL), lambda b, t: (layer, 0, col_block), pipeline_mode=pl.Buffered(1))


def _chunks(w_all, layer, first_block, count):
    specs = [_chunk_spec(layer, first_block + c, w_all.shape[1]) for c in range(count)]
    return [w_all] * count, specs


def _layer_p_spec(layer, rows):
    return pl.BlockSpec((None, 1, rows, PLE_DIM), lambda b, t: (layer, b, t, 0))


def _ssm_layer(h, p, layer, in_w16, out_w16, gate_w16, proj_w16, nw, wdt, cw, cb, dtb, alog, dsk, gnw, pnw):
    bsz, seq, _ = h.shape
    wz, wz_specs = _chunks(in_w16, layer, 0, D_INNER // WCOL)
    wxbc, wxbc_specs = _chunks(in_w16, layer, D_INNER // WCOL, CONV_CH // WCOL)
    ow, ow_specs = _chunks(out_w16, layer, 0, D_MODEL // WCOL)
    pgw, pgw_specs = _chunks(gate_w16, layer, 0, D_MODEL // WCOL)
    ppw, ppw_specs = _chunks(proj_w16, layer, 0, D_MODEL // WCOL)
    small = lambda *ws: [_const_spec(w.shape) for w in ws]
    weights = (nw, *wz, *wxbc, wdt, cw, cb, dtb, alog, dsk, gnw, *ow, pnw, *pgw, *ppw)
    weight_specs = (small(nw) + wz_specs + wxbc_specs + small(wdt, cw, cb, dtb, alog, dsk, gnw) + ow_specs
                    + small(pnw) + pgw_specs + ppw_specs)
    return pl.pallas_call(
        _ssm_kernel,
        grid=(bsz, seq // STS),
        in_specs=[pl.BlockSpec((1, STS, D_MODEL), lambda b, t: (b, t, 0)), _layer_p_spec(layer, STS)]
                 + weight_specs,
        out_specs=pl.BlockSpec((1, STS, D_MODEL), lambda b, t: (b, t, 0)),
        out_shape=jax.ShapeDtypeStruct(h.shape, F32),
        scratch_shapes=[pltpu.VMEM((CONV_CH // LANES, CTAIL + STS, LANES), F32),
                        pltpu.VMEM((STS, D_MODEL), BF16),
                        pltpu.VMEM((STS, LANES), F32),
                        pltpu.VMEM((D_INNER // LANES, STS, LANES), F32),
                        pltpu.VMEM((STS, D_INNER), BF16),
                        pltpu.VMEM((D_MODEL // LANES, STS, LANES), F32),
                        pltpu.VMEM((SSM_GROUPS, D_STATE, GW), F32)],
        compiler_params=pltpu.CompilerParams(dimension_semantics=("arbitrary", "arbitrary"),
                                             vmem_limit_bytes=VMEM_LIMIT),
        name="ssm_layer",
    )(h, p, *weights)


def _kv_kernel(x_ref, pos_ref, nw_ref, wk_ref, wv_ref, knw_ref, invf_ref, seg_ref,
               kt_ref, v_ref, cos_ref, sin_ref):
    u = _rms(x_ref[0], nw_ref[...]).astype(BF16)
    ang = pos_ref[0] * invf_ref[...]
    cos = jnp.cos(ang)
    sin = jnp.sin(ang)
    cos_ref[0] = cos
    sin_ref[0] = sin
    kk = _dot(u, wk_ref[...])
    k1, k2 = kk[:, :LANES], kk[:, LANES:]
    zhi, zlo = _split2(k1 * k1 + k2 * k2)
    ssum = _dot(zhi, seg_ref[...]) + _dot(zlo, seg_ref[...])
    rs = lax.rsqrt(ssum * (1.0 / HEAD_DIM) + EPS)
    n1 = k1 * rs * knw_ref[:, :LANES]
    n2 = k2 * rs * knw_ref[:, LANES:]
    kr = jnp.concatenate([n1 * cos - n2 * sin, n2 * cos + n1 * sin], axis=1)
    kt_ref[0] = kr.T.astype(BF16)
    v_ref[0] = _dot(u, wv_ref[...]).astype(BF16)


def _shared_kv(h, posf, nw, wk, wv, knw, invf, seg):
    bsz, seq, _ = h.shape
    consts = (nw, wk, wv, knw, invf, seg)
    return pl.pallas_call(
        _kv_kernel,
        grid=(bsz, seq // TS),
        in_specs=[pl.BlockSpec((1, TS, D_MODEL), lambda b, t: (b, t, 0)),
                  pl.BlockSpec((1, TS, 1), lambda b, t: (b, t, 0))]
                 + [_const_spec(w.shape) for w in consts],
        out_specs=[pl.BlockSpec((1, KV_WIDTH, TS), lambda b, t: (b, 0, t)),
                   pl.BlockSpec((1, TS, 2 * KV_WIDTH), lambda b, t: (b, t, 0)),
                   pl.BlockSpec((1, TS, LANES), lambda b, t: (b, t, 0)),
                   pl.BlockSpec((1, TS, LANES), lambda b, t: (b, t, 0))],
        out_shape=[jax.ShapeDtypeStruct((bsz, KV_WIDTH, seq), BF16),
                   jax.ShapeDtypeStruct((bsz, seq, 2 * KV_WIDTH), BF16),
                   jax.ShapeDtypeStruct((bsz, seq, LANES), F32),
                   jax.ShapeDtypeStruct((bsz, seq, LANES), F32)],
        compiler_params=pltpu.CompilerParams(dimension_semantics=("arbitrary", "arbitrary"),
                                             vmem_limit_bytes=VMEM_LIMIT),
        name="shared_kv",
    )(h, posf, *consts)


_ATTN_REF_COUNTS = (None,) * 10 + (ATT_WIDTH // WCOL, ATT_WIDTH // WCOL, None, None, D_MODEL // WCOL, None,
                                   D_MODEL // WCOL, D_MODEL // WCOL) + (None,) * 7


def _attn_kernel(*refs):
    (sink_ref, x_ref, p_ref, cos_ref, sin_ref, ktc_ref, ktp_ref, vc_ref, vp_ref, nw_ref, wq_ref, wg_ref, qnw_ref,
     seg_ref, ow_ref, pnw_ref, pgw_ref, ppw_ref, o_ref, u_s, q16_s, krhs_s, gs_s, o_s, h2_s) = _take(refs, _ATTN_REF_COUNTS)
    t = pl.program_id(1)
    qw = ATT_WIDTH // 2

    @pl.when(t == 0)
    def _():
        krhs_s[...] = jnp.zeros(krhs_s.shape, BF16)

    u_s[...] = _rms(x_ref[0], nw_ref[...]).astype(BF16)

    ii = lax.broadcasted_iota(jnp.int32, (BLOCK, BLOCK), 0)
    jj = lax.broadcasted_iota(jnp.int32, (BLOCK, BLOCK), 1)
    own_ok = jj <= ii
    first_off = jnp.where(t > 0, 0, BLOCK)
    valid_first = jnp.concatenate([jj > ii + first_off, own_ok], axis=1)
    valid_rest = jnp.concatenate([jj > ii, own_ok], axis=1)
    lo_half = lax.broadcasted_iota(jnp.int32, (2 * BLOCK, LANES), 1) < HEAD_DIM
    scale = HEAD_DIM ** -0.5

    def q_item(b):
        r0 = b * BLOCK
        u = u_s[r0:r0 + BLOCK, :]
        q1 = _dot(u, wq_ref[0][...])
        q2 = _dot(u, wq_ref[1][...])
        zhi, zlo = _split2(q1 * q1 + q2 * q2)
        sw = seg_ref.shape[0]
        ssum = jnp.concatenate([_dot(zhi[:, c:c + sw], seg_ref[...]) + _dot(zlo[:, c:c + sw], seg_ref[...])
                                for c in range(0, qw, sw)], axis=1)
        rs = lax.rsqrt(ssum * (1.0 / HEAD_DIM) + EPS)
        n1 = q1 * rs * qnw_ref[:, :qw]
        n2 = q2 * rs * qnw_ref[:, qw:]
        cos = jnp.concatenate([cos_ref[0, r0:r0 + BLOCK, :]] * (qw // LANES), axis=1)
        sin = jnp.concatenate([sin_ref[0, r0:r0 + BLOCK, :]] * (qw // LANES), axis=1)
        q16_s[r0:r0 + BLOCK, :qw] = ((n1 * cos - n2 * sin) * scale).astype(BF16)
        q16_s[r0:r0 + BLOCK, qw:] = ((n2 * cos + n1 * sin) * scale).astype(BF16)

    def gate_item(b, k):
        r0 = b * BLOCK
        gs_s[r0:r0 + BLOCK, k * WCOL:(k + 1) * WCOL] = _silu(_dot(u_s[r0:r0 + BLOCK, :], wg_ref[k][...]))

    def kv_blocks(b):
        r0 = b * BLOCK
        if b == 0:
            return ktp_ref[0], ktc_ref[0, :, 0:BLOCK], vp_ref[0], vc_ref[0, 0:BLOCK, :]
        return (ktc_ref[0, :, r0 - BLOCK:r0], ktc_ref[0, :, r0:r0 + BLOCK],
                vc_ref[0, r0 - BLOCK:r0, :], vc_ref[0, r0:r0 + BLOCK, :])

    scores = {}

    def score_item(b, g):
        r0 = b * BLOCK
        slot = (b * KV_HEADS + g) % 2
        kprev, kown, _, _ = kv_blocks(b)
        for hl in range(Q_PER_KV):
            for half in range(2):
                src = half * LANES + g * HALF
                dst = half * LANES + hl * HALF
                krhs_s[slot, dst:dst + HALF, hl * 2 * BLOCK:hl * 2 * BLOCK + BLOCK] = kprev[src:src + HALF, :]
                krhs_s[slot, dst:dst + HALF, hl * 2 * BLOCK + BLOCK:(hl + 1) * 2 * BLOCK] = kown[src:src + HALF, :]
        lhs = jnp.concatenate([q16_s[r0:r0 + BLOCK, g * LANES:(g + 1) * LANES],
                               q16_s[r0:r0 + BLOCK, qw + g * LANES:qw + (g + 1) * LANES]], axis=1)
        scores[(b, g)] = _dot(lhs, krhs_s[slot])

    def softmax_pv_item(b, g):
        r0 = b * BLOCK
        s_all = scores.pop((b, g))
        valid = valid_first if b == 0 else valid_rest
        _, _, vprev, vown = kv_blocks(b)
        vg = jnp.concatenate([vprev[:, g * LANES:(g + 1) * LANES], vown[:, g * LANES:(g + 1) * LANES]], axis=0)
        zero = jnp.zeros_like(vg)
        vbd = jnp.concatenate([jnp.where(lo_half, vg, zero), jnp.where(lo_half, zero, vg)], axis=0)
        probs = []
        for hl in range(Q_PER_KV):
            sink = sink_ref[g * Q_PER_KV + hl]
            s = jnp.where(valid, s_all[:, hl * 2 * BLOCK:(hl + 1) * 2 * BLOCK], -jnp.inf)
            m = jnp.maximum(jnp.max(s, axis=-1, keepdims=True), sink)
            e = jnp.exp(s - m)
            den = jnp.sum(e, axis=-1, keepdims=True) + jnp.exp(sink - m)
            probs.append((e * (1.0 / den)).astype(BF16))
        for pr in range(Q_PER_KV // 2):
            lhs_p = jnp.concatenate([probs[2 * pr], probs[2 * pr + 1]], axis=1)
            c0 = (g * 2 + pr) * LANES
            o_s[r0:r0 + BLOCK, c0:c0 + LANES] = _dot(lhs_p, vbd)

    def out_item(b):
        r0 = b * BLOCK
        og = (o_s[r0:r0 + BLOCK, :] * gs_s[r0:r0 + BLOCK, :]).astype(BF16)
        h2_s[r0:r0 + BLOCK, :] = x_ref[0, r0:r0 + BLOCK, :] + _dotw(og, ow_ref)

    def ple_item(b):
        r0 = b * BLOCK
        o_ref[0, r0:r0 + BLOCK, :] = _ple(h2_s[r0:r0 + BLOCK, :], p_ref[0, r0:r0 + BLOCK, :],
                                          pnw_ref, pgw_ref, ppw_ref)

    q_item(0)
    score_item(0, 0)
    for b in range(ANB):
        more = b + 1 < ANB
        for g in range(KV_HEADS):
            if g + 1 < KV_HEADS:
                score_item(b, g + 1)
            elif more:
                score_item(b + 1, 0)
            if g == 0:
                gate_item(b, 0)
                gate_item(b, 1)
            elif g == 1 and b > 0:
                out_item(b - 1)
            elif g == 2 and more:
                q_item(b + 1)
            elif g == 3 and b > 0:
                ple_item(b - 1)
            softmax_pv_item(b, g)
    out_item(ANB - 1)
    ple_item(ANB - 1)


def _attn_layer(h, p, layer, j, sinks, cos, sin, kt, v2, q_w16, in_w16, out_w16, gate_w16, proj_w16, nw, qnw, seg, pnw):
    bsz, seq, _ = h.shape
    n_half = ATT_WIDTH // WCOL
    wq, wq_specs = _chunks(q_w16, j, 0, n_half)
    wg, wg_specs = _chunks(in_w16, j, n_half, n_half)
    ow, ow_specs = _chunks(out_w16, j, 0, D_MODEL // WCOL)
    pgw, pgw_specs = _chunks(gate_w16, layer, 0, D_MODEL // WCOL)
    ppw, ppw_specs = _chunks(proj_w16, layer, 0, D_MODEL // WCOL)
    small = lambda *ws: [_const_spec(w.shape) for w in ws]
    consts = (nw, *wq, *wg, qnw, seg, *ow, pnw, *pgw, *ppw)
    const_specs = small(nw) + wq_specs + wg_specs + small(qnw, seg) + ow_specs + small(pnw) + pgw_specs + ppw_specs
    prev_blk = lambda t: jnp.maximum(t * ANB - 1, 0)
    return pl.pallas_call(
        _attn_kernel,
        grid=(bsz, seq // ATS),
        in_specs=[pl.BlockSpec(memory_space=pltpu.SMEM),
                  pl.BlockSpec((1, ATS, D_MODEL), lambda b, t: (b, t, 0)),
                  _layer_p_spec(layer, ATS),
                  pl.BlockSpec((1, ATS, LANES), lambda b, t: (b, t, 0)),
                  pl.BlockSpec((1, ATS, LANES), lambda b, t: (b, t, 0)),
                  pl.BlockSpec((1, KV_WIDTH, ATS), lambda b, t: (b, 0, t)),
                  pl.BlockSpec((1, KV_WIDTH, BLOCK), lambda b, t: (b, 0, prev_blk(t))),
                  pl.BlockSpec((1, ATS, 2 * KV_WIDTH), lambda b, t: (b, t, 0)),
                  pl.BlockSpec((1, BLOCK, 2 * KV_WIDTH), lambda b, t: (b, prev_blk(t), 0))]
                 + const_specs,
        out_specs=pl.BlockSpec((1, ATS, D_MODEL), lambda b, t: (b, t, 0)),
        out_shape=jax.ShapeDtypeStruct(h.shape, F32),
        scratch_shapes=[pltpu.VMEM((ATS, D_MODEL), BF16),
                        pltpu.VMEM((ATS, ATT_WIDTH), BF16),
                        pltpu.VMEM((2, 2 * LANES, Q_PER_KV * 2 * BLOCK), BF16),
                        pltpu.VMEM((ATS, ATT_WIDTH), F32),
                        pltpu.VMEM((ATS, ATT_WIDTH), F32),
                        pltpu.VMEM((ATS, D_MODEL), F32)],
        compiler_params=pltpu.CompilerParams(dimension_semantics=("arbitrary", "arbitrary"),
                                             vmem_limit_bytes=VMEM_LIMIT),
        name="attn_layer",
    )(sinks, h, p, cos, sin, kt, kt, v2, v2, *consts)


def _row(v):
    return v.reshape(1, -1).astype(F32)


def _pad_lanes(v):
    return jnp.pad(v, [(0, 0)] * (v.ndim - 1) + [(0, LANES - v.shape[-1])])


def _halves_perm(n_heads):
    d = np.arange(HALF)
    first = (np.arange(n_heads)[:, None] * HEAD_DIM + d[None, :]).reshape(-1)
    return np.concatenate([first, first + HALF])


def _segment_ones(width):
    idx = np.arange(width) // HALF
    return jnp.asarray(idx[:, None] == idx[None, :], dtype=BF16)


def kernel(x, p, positions, ssm_norm_w, ssm_in_w, ssm_conv_w, ssm_conv_b, ssm_dt_bias, ssm_a_log, ssm_d,
           ssm_gnorm_w, ssm_out_w, kv_norm_w, kv_w, k_norm_w, attn_norm_w, attn_in_w, q_norm_w, attn_sinks,
           attn_out_w, ple_norm_w, ple_gate_w, ple_proj_w):
    n_a = ssm_in_w.shape[0]
    n_b = attn_in_w.shape[0]
    head_order = np.concatenate([np.arange(0, SSM_HEADS, 2), np.arange(1, SSM_HEADS, 2)])
    ssm_in_w16 = ssm_in_w.astype(BF16)
    ssm_out_w16 = ssm_out_w.astype(BF16)
    gate_w16 = ple_gate_w.astype(BF16)
    proj_w16 = ple_proj_w.astype(BF16)
    h = x
    for i in range(n_a):
        h = _ssm_layer(
            h, p, i, ssm_in_w16, ssm_out_w16, gate_w16, proj_w16, _row(ssm_norm_w[i]),
            _pad_lanes(ssm_in_w[i][:, D_INNER + CONV_CH:][:, head_order]).astype(BF16),
            ssm_conv_w[i].astype(F32), _row(ssm_conv_b[i]),
            _pad_lanes(_row(ssm_dt_bias[i][head_order])), _pad_lanes(_row(ssm_a_log[i][head_order])),
            _row(jnp.repeat(ssm_d[i], SSM_HEAD_DIM)), _row(ssm_gnorm_w[i]), _row(ple_norm_w[i]))

    inv_freq = ROPE_THETA ** (-(jnp.arange(HALF, dtype=F32) * 2.0 / HEAD_DIM))
    kperm = _halves_perm(KV_HEADS)
    qperm = _halves_perm(ATT_HEADS)
    posf = positions.astype(F32)[..., None]
    wv = kv_w[:, KV_WIDTH:].reshape(D_MODEL, KV_HEADS, 1, HEAD_DIM)
    wv = jnp.broadcast_to(wv, (D_MODEL, KV_HEADS, 2, HEAD_DIM)).reshape(D_MODEL, 2 * KV_WIDTH)
    kt, v2, cos, sin = _shared_kv(
        h, posf, _row(kv_norm_w), kv_w[:, :KV_WIDTH][:, kperm].astype(BF16), wv.astype(BF16),
        _row(jnp.tile(k_norm_w.reshape(2, HALF), (1, KV_HEADS))),
        _row(jnp.tile(inv_freq, LANES // HALF)), _segment_ones(LANES))

    seg_q = _segment_ones(2 * LANES)
    attn_in_w16 = attn_in_w.astype(BF16)
    attn_q_w16 = attn_in_w[:, :, :ATT_WIDTH][:, :, qperm].astype(BF16)
    attn_out_w16 = attn_out_w.astype(BF16)
    for j in range(n_b):
        i = n_a + j
        h = _attn_layer(
            h, p, i, j, attn_sinks[j].astype(F32), cos, sin, kt, v2,
            attn_q_w16, attn_in_w16, attn_out_w16, gate_w16, proj_w16, _row(attn_norm_w[j]),
            _row(jnp.tile(q_norm_w[j].reshape(2, HALF), (1, ATT_HEADS))), seg_q, _row(ple_norm_w[i]))
    return h
```

```python
import functools

import jax
import jax.numpy as jnp
import numpy as np
from jax import lax
from jax.experimental import pallas as pl
from jax.experimental.pallas import tpu as pltpu

F32 = jnp.float32
BF16 = jnp.bfloat16

D_MODEL = 1024
D_INNER = 2048
SSM_HEAD_DIM = 64
SSM_HEADS = 32
SSM_GROUPS = 4
SSM_HPG = 8
D_STATE = 128
CONV_WIDTH = 4
BC_WIDTH = SSM_GROUPS * D_STATE
CONV_CH = D_INNER + 2 * BC_WIDTH
CHUNK = 128
ATT_HEADS = 16
KV_HEADS = 4
HEAD_DIM = 64
HALF = HEAD_DIM // 2
Q_PER_KV = 4
ATT_WIDTH = 1024
KV_WIDTH = 256
BLOCK = 128
ROPE_THETA = 10000.0
PLE_DIM = 256
EPS = 1e-6

LANES = 128
SUBLANES = 8
TS = 256
ATS = 512
ANB = ATS // BLOCK
STS = 512
SNCH = STS // CHUNK
MROWS = 256
VREGS_PER_CHUNK = CHUNK // SUBLANES
CTAIL = (CONV_WIDTH - 1) * SUBLANES
WCOL = 512
GW = D_INNER // SSM_GROUPS
PAIRS = SSM_HEADS // 2
PPG = SSM_HPG // 2
VMEM_LIMIT = 56 * 1024 * 1024

_NT = (((1,), (1,)), ((), ()))


def _dot(a, b):
    return jnp.dot(a, b, preferred_element_type=F32)


def _dotw(a, w_refs):
    return jnp.concatenate([_dot(a, w[...]) for w in w_refs], axis=1)


def _rms(x, w):
    ms = jnp.mean(x * x, axis=-1, keepdims=True)
    return x * lax.rsqrt(ms + EPS) * w


def _sigmoid(x):
    return 0.5 * jnp.tanh(0.5 * x) + 0.5


def _silu(x):
    hx = 0.5 * x
    return hx * jnp.tanh(hx) + hx


def _split2(x):
    hi = x.astype(BF16)
    lo = (x - hi.astype(F32)).astype(BF16)
    return hi, lo


def _ple(h2, p_tile, pnw_ref, pgw_ref, ppw_ref):
    u2 = _rms(h2, pnw_ref[...]).astype(BF16)
    gate = _sigmoid(_dotw(u2, pgw_ref))
    proj = _dotw(p_tile.astype(BF16), ppw_ref)
    return h2 + gate * proj


def _time_of_row(r):
    return (r & (SUBLANES - 1)) * VREGS_PER_CHUNK + (r >> 3)


def _scatter_rows(ref, tile0, row0, val, to_kernel_order):
    for tile in range(val.shape[1] // LANES):
        for i in range(val.shape[0] // SUBLANES):
            chunk, v = divmod(i, VREGS_PER_CHUNK)
            if to_kernel_order:
                rows = pl.ds(row0 + chunk * CHUNK + (CHUNK // 2) * (v % 2) + v // 2, SUBLANES, stride=SUBLANES)
            else:
                rows = pl.ds(row0 + chunk * CHUNK + v, SUBLANES, stride=VREGS_PER_CHUNK)
            piece = val[i * SUBLANES:(i + 1) * SUBLANES, tile * LANES:(tile + 1) * LANES]
            if len(ref.shape) == 3:
                ref[pl.ds(tile0 + tile, 1), rows, :] = piece[None]
            else:
                ref[rows, :] = piece


def _gather_tiles(ref, tile0, n_tiles, rows):
    return jnp.concatenate([ref[tile0 + k, rows, :] for k in range(n_tiles)], axis=1)


def _take(refs, counts):
    out, i = [], 0
    for n in counts:
        out.append(refs[i] if n is None else refs[i:i + n])
        i += 1 if n is None else n
    assert i == len(refs)
    return out


_SSM_REF_COUNTS = (None, None, None, None, D_INNER // WCOL, CONV_CH // WCOL, None, None, None, None, None, None, None,
                   D_MODEL // WCOL, None, D_MODEL // WCOL, D_MODEL // WCOL) + (None,) * 9


def _ssm_kernel(*refs, tiles_per_row, n_tiles):
    (x_ref, xp_ref, pp_ref, nw_ref, wz_ref, wxbc_ref, wdt_ref, cw_ref, cb_ref, dtb_ref, alog_ref, dsk_ref, gnw_ref,
     ow_ref, pnw_ref, pgw_ref, ppw_ref, o_ref, cbuf, u_s, dt_s, zs_s, y16_s, y16p_s, h2_s, st_s) = _take(
        refs, _SSM_REF_COUNTS)
    s = pl.program_id(0)
    row_start = s % tiles_per_row == 0

    @pl.when(s == 0)
    def _():
        y16p_s[...] = jnp.zeros(y16p_s.shape, BF16)

    @pl.when(row_start)
    def _():
        cbuf[:, 0:CTAIL, :] = jnp.zeros((CONV_CH // LANES, CTAIL, LANES), F32)
        st_s[...] = jnp.zeros(st_s.shape, F32)

    @pl.when(jnp.logical_not(row_start))
    def _():
        cbuf[:, 0:CTAIL, :] = cbuf[:, STS:STS + CTAIL, :]

    ii = lax.broadcasted_iota(jnp.int32, (CHUNK, LANES), 0)
    ll = lax.broadcasted_iota(jnp.int32, (CHUNK, LANES), 1)
    lo_half = ll < SSM_HEAD_DIM
    time_i = _time_of_row(ii)
    time_lo = _time_of_row(jnp.where(lo_half, ll, ll - SSM_HEAD_DIM))
    mask_lo = time_lo <= time_i
    mask_hi = time_lo + SUBLANES <= time_i
    ltri = (_time_of_row(ll) <= time_i).astype(F32)
    lo_row = lax.broadcasted_iota(jnp.int32, (PAIRS, LANES), 1) < SSM_HEAD_DIM
    lo_k = lax.broadcasted_iota(jnp.int32, (SSM_HEAD_DIM, LANES), 1) < SSM_HEAD_DIM

    tabs = []

    def norm_dt_item():
        u_s[...] = _rms(x_ref[0], nw_ref[...]).astype(BF16)
        _scatter_rows(dt_s, 0, 0, _dot(u_s[...], wdt_ref[...]) + dtb_ref[...], True)
        dt_raw = dt_s[...]
        dt = jnp.maximum(dt_raw, 0.0) + jnp.log1p(jnp.exp(-jnp.abs(dt_raw)))
        d_a = dt * (-jnp.exp(alog_ref[...]))
        for c in range(SNCH):
            r0 = c * CHUNK
            dt_c = dt[r0:r0 + CHUNK]
            acum = jnp.dot(ltri, d_a[r0:r0 + CHUNK], precision=lax.Precision.HIGHEST,
                           preferred_element_type=F32)
            acum_t = acum.T
            dt_t = dt_c.T
            ev, od = acum_t[:PAIRS], acum_t[PAIRS:2 * PAIRS]
            dev, dod = dt_t[:PAIRS], dt_t[PAIRS:2 * PAIRS]
            tabs.append(dict(
                acum=acum,
                w_out=jnp.exp(acum[CHUNK - 1:CHUNK, :] - acum) * dt_c,
                a_lo=jnp.where(lo_row, ev, pltpu.roll(od, SSM_HEAD_DIM, 1)),
                a_hi=jnp.where(lo_row, pltpu.roll(ev, SSM_HEAD_DIM, 1), od),
                d_lo=jnp.where(lo_row, dev, pltpu.roll(dod, SSM_HEAD_DIM, 1)),
                d_hi=jnp.where(lo_row, pltpu.roll(dev, SSM_HEAD_DIM, 1), dod)))

    def pair_expand(v, pair):
        idx = jnp.where(lo_half, pair, PAIRS + pair)
        return jnp.take_along_axis(v, idx, axis=1)

    tiles_per_wcol = WCOL // LANES

    def proj_item(hf, j):
        r0 = hf * MROWS
        _scatter_rows(cbuf, j * tiles_per_wcol, CTAIL + r0, _dot(u_s[r0:r0 + MROWS, :], wxbc_ref[j][...]), True)

    def z_item(hf, g):
        r0 = hf * MROWS
        _scatter_rows(zs_s, g * tiles_per_wcol, r0, _silu(_dot(u_s[r0:r0 + MROWS, :], wz_ref[g][...])), True)

    def conv(r0, c0, width):
        base = CTAIL + r0
        tile0, n_tiles = c0 // LANES, width // LANES
        x = _gather_tiles(cbuf, tile0, n_tiles, slice(base, base + CHUNK))
        last = lax.broadcasted_iota(jnp.int32, (SUBLANES, width), 0) == SUBLANES - 1
        wrapped = []
        for v in range(VREGS_PER_CHUNK - CONV_WIDTH + 1, VREGS_PER_CHUNK):
            prev = _gather_tiles(cbuf, tile0, n_tiles,
                                 slice(base - CHUNK + v * SUBLANES, base - CHUNK + (v + 1) * SUBLANES))
            wrapped.append(pltpu.roll(jnp.where(last, prev, x[v * SUBLANES:(v + 1) * SUBLANES]), 1, 0))
        acc = cb_ref[:, c0:c0 + width] + cw_ref[CONV_WIDTH - 1:CONV_WIDTH, c0:c0 + width] * x
        for k in range(1, CONV_WIDTH):
            shifted = jnp.concatenate(wrapped[CONV_WIDTH - 1 - k:] + [x[:CHUNK - k * SUBLANES]], axis=0)
            acc = acc + cw_ref[CONV_WIDTH - 1 - k:CONV_WIDTH - k, c0:c0 + width] * shifted
        return _silu(acc)

    prepared = {}

    def ssd_prepare(c, g):
        r0, gc, tab = c * CHUNK, g * GW, tabs[c]
        xs = conv(r0, gc, GW)
        bm = conv(r0, D_INNER + g * D_STATE, D_STATE)
        cm = conv(r0, D_INNER + BC_WIDTH + g * D_STATE, D_STATE)
        cm16, bm16 = cm.astype(BF16), bm.astype(BF16)
        h0, h1 = bm16[:SSM_HEAD_DIM], bm16[SSM_HEAD_DIM:]
        cb2 = lax.dot_general(cm16, jnp.concatenate([h0, h0, h1, h1], axis=0), _NT,
                              preferred_element_type=F32)
        cb_lo, cb_hi = cb2[:, :LANES], cb2[:, LANES:]
        a_exp = [pair_expand(tab["acum"], g * PPG + q) for q in range(PPG)]
        w_exp = jnp.concatenate([pair_expand(tab["w_out"], g * PPG + q) for q in range(PPG)], axis=1)
        a_all = jnp.concatenate(a_exp, axis=1)
        st = st_s[g]
        y_off = _dot(cm16, st.astype(BF16))
        st_s[g] = st * jnp.exp(a_all[CHUNK - 1:CHUNK, :]) + _dot(bm.T.astype(BF16), (xs * w_exp).astype(BF16))

        operands = []
        for q in range(PPG):
            pair = g * PPG + q
            t_lo = cb_lo * jnp.exp(jnp.where(mask_lo, a_exp[q] - tab["a_lo"][pair:pair + 1, :], -jnp.inf))
            t_hi = cb_hi * jnp.exp(jnp.where(mask_hi, a_exp[q] - tab["a_hi"][pair:pair + 1, :], -jnp.inf))
            lhs = jnp.concatenate([(t_lo * tab["d_lo"][pair:pair + 1, :]).astype(BF16),
                                   (t_hi * tab["d_hi"][pair:pair + 1, :]).astype(BF16)], axis=1)
            xs16 = xs[:, q * LANES:(q + 1) * LANES].astype(BF16)
            x0, x1 = xs16[:SSM_HEAD_DIM], xs16[SSM_HEAD_DIM:]
            zero = jnp.zeros_like(x0)
            rhs = jnp.concatenate([jnp.where(lo_k, x0, zero), jnp.where(lo_k, zero, x0),
                                   jnp.where(lo_k, x1, zero), jnp.where(lo_k, zero, x1)], axis=0)
            operands.append((lhs, rhs))
        prepared[(c, g)] = operands, jnp.exp(a_all) * y_off + dsk_ref[:, gc:gc + GW] * xs

    def ssd_finish(c, g):
        r0, gc = c * CHUNK, g * GW
        operands, rest = prepared.pop((c, g))
        y = jnp.concatenate([_dot(lhs, rhs) for lhs, rhs in operands], axis=1) + rest
        y = y * _gather_tiles(zs_s, gc // LANES, GW // LANES, slice(r0, r0 + CHUNK))
        y = y * lax.rsqrt(jnp.mean(y * y, axis=-1, keepdims=True) + EPS) * gnw_ref[:, gc:gc + GW]
        y16_s[r0:r0 + CHUNK, gc:gc + GW] = y.astype(BF16)

    def out_item(hf, k):
        r0 = hf * MROWS
        _scatter_rows(h2_s, k * tiles_per_wcol, r0, _dot(y16p_s[r0:r0 + MROWS, :], ow_ref[k][...]), False)

    def ple_item(hf):
        r0 = hf * MROWS
        h2 = xp_ref[0, r0:r0 + MROWS, :] + _gather_tiles(h2_s, 0, D_MODEL // LANES, slice(r0, r0 + MROWS))
        o_ref[0, r0:r0 + MROWS, :] = _ple(h2, pp_ref[0, r0:r0 + MROWS, :], pnw_ref, pgw_ref, ppw_ref)

    def keep_item():
        y16p_s[...] = y16_s[...]

    n_, p_, z_, a_, f_, o_, g_, k_ = (norm_dt_item, proj_item, z_item, ssd_prepare, ssd_finish, out_item,
                                      ple_item, keep_item)
    bc = (D_INNER // WCOL, D_INNER // WCOL + 1)
    order = [(n_,), (p_, 0, bc[0]), (p_, 0, bc[1]), (p_, 0, 0),
             (a_, 0, 0), (z_, 0, 0), (f_, 0, 0), (p_, 0, 1),
             (a_, 0, 1), (z_, 0, 1), (f_, 0, 1), (p_, 0, 2),
             (a_, 0, 2), (z_, 0, 2), (f_, 0, 2), (p_, 0, 3),
             (a_, 0, 3), (z_, 0, 3), (f_, 0, 3), (p_, 1, bc[0]),
             (a_, 1, 0), (p_, 1, bc[1]), (f_, 1, 0), (p_, 1, 0),
             (a_, 1, 1), (z_, 1, 0), (f_, 1, 1), (p_, 1, 1),
             (a_, 1, 2), (z_, 1, 1), (f_, 1, 2), (p_, 1, 2),
             (a_, 1, 3), (z_, 1, 2), (f_, 1, 3), (p_, 1, 3),
             (a_, 2, 0), (z_, 1, 3), (f_, 2, 0), (o_, 0, 0),
             (a_, 2, 1), (o_, 0, 1), (f_, 2, 1),
             (a_, 2, 2), (f_, 2, 2), (g_, 0),
             (a_, 2, 3), (o_, 1, 0), (f_, 2, 3),
             (a_, 3, 0), (f_, 3, 0), (o_, 1, 1),
             (a_, 3, 1), (f_, 3, 1),
             (a_, 3, 2), (g_, 1), (f_, 3, 2),
             (a_, 3, 3), (f_, 3, 3), (k_,)]
    flush = [(o_, 0, 0), (o_, 0, 1), (g_, 0), (o_, 1, 0), (o_, 1, 1), (g_, 1)]
    assert SNCH == 4 and STS == 2 * MROWS

    @pl.when(s < n_tiles)
    def _():
        for item, *args in order:
            item(*args)

    @pl.when(s == n_tiles)
    def _():
        for item, *args in flush:
            item(*args)


def _const_spec(shape):
    nd = len(shape)
    return pl.BlockSpec(shape, lambda *_: (0,) * nd, pipeline_mode=pl.Buffered(1))


def _chunk_spec(layer, col_block, rows):
    return pl.BlockSpec((None, rows, WCOL), lambda *_: (layer, 0, col_block), pipeline_mode=pl.Buffered(1))


def _chunks(w_all, layer, first_block, count):
    specs = [_chunk_spec(layer, first_block + c, w_all.shape[1]) for c in range(count)]
    return [w_all] * count, specs


def _layer_p_spec(layer, rows):
    return pl.BlockSpec((None, 1, rows, PLE_DIM), lambda b, t: (layer, b, t, 0))


def _ssm_layer(h, p, layer, in_w16, out_w16, gate_w16, proj_w16, nw, wdt, cw, cb, dtb, alog, dsk, gnw, pnw):
    bsz, seq, _ = h.shape
    wz, wz_specs = _chunks(in_w16, layer, 0, D_INNER // WCOL)
    wxbc, wxbc_specs = _chunks(in_w16, layer, D_INNER // WCOL, CONV_CH // WCOL)
    ow, ow_specs = _chunks(out_w16, layer, 0, D_MODEL // WCOL)
    pgw, pgw_specs = _chunks(gate_w16, layer, 0, D_MODEL // WCOL)
    ppw, ppw_specs = _chunks(proj_w16, layer, 0, D_MODEL // WCOL)
    small = lambda *ws: [_const_spec(w.shape) for w in ws]
    weights = (nw, *wz, *wxbc, wdt, cw, cb, dtb, alog, dsk, gnw, *ow, pnw, *pgw, *ppw)
    weight_specs = (small(nw) + wz_specs + wxbc_specs + small(wdt, cw, cb, dtb, alog, dsk, gnw) + ow_specs
                    + small(pnw) + pgw_specs + ppw_specs)
    nt = seq // STS
    n_tiles = bsz * nt

    def this_tile(s):
        n = jnp.minimum(s, n_tiles - 1)
        return n // nt, n % nt

    def prev_tile(s):
        n = jnp.maximum(s - 1, 0)
        return n // nt, n % nt

    return pl.pallas_call(
        functools.partial(_ssm_kernel, tiles_per_row=nt, n_tiles=n_tiles),
        grid=(n_tiles + 1,),
        in_specs=[pl.BlockSpec((1, STS, D_MODEL), lambda s: (*this_tile(s), 0)),
                  pl.BlockSpec((1, STS, D_MODEL), lambda s: (*prev_tile(s), 0)),
                  pl.BlockSpec((None, 1, STS, PLE_DIM), lambda s: (layer, *prev_tile(s), 0))]
                 + weight_specs,
        out_specs=pl.BlockSpec((1, STS, D_MODEL), lambda s: (*prev_tile(s), 0)),
        out_shape=jax.ShapeDtypeStruct(h.shape, F32),
        scratch_shapes=[pltpu.VMEM((CONV_CH // LANES, CTAIL + STS, LANES), F32),
                        pltpu.VMEM((STS, D_MODEL), BF16),
                        pltpu.VMEM((STS, LANES), F32),
                        pltpu.VMEM((D_INNER // LANES, STS, LANES), F32),
                        pltpu.VMEM((STS, D_INNER), BF16),
                        pltpu.VMEM((STS, D_INNER), BF16),
                        pltpu.VMEM((D_MODEL // LANES, STS, LANES), F32),
                        pltpu.VMEM((SSM_GROUPS, D_STATE, GW), F32)],
        compiler_params=pltpu.CompilerParams(dimension_semantics=("arbitrary",),
                                             vmem_limit_bytes=VMEM_LIMIT),
        name="ssm_layer",
    )(h, h, p, *weights)


def _kv_kernel(x_ref, pos_ref, nw_ref, wk_ref, wv_ref, knw_ref, invf_ref, seg_ref,
               kt_ref, v_ref, cos_ref, sin_ref):
    u = _rms(x_ref[0], nw_ref[...]).astype(BF16)
    ang = pos_ref[0] * invf_ref[...]
    cos = jnp.cos(ang)
    sin = jnp.sin(ang)
    cos_ref[0] = cos
    sin_ref[0] = sin
    kk = _dot(u, wk_ref[...])
    k1, k2 = kk[:, :LANES], kk[:, LANES:]
    zhi, zlo = _split2(k1 * k1 + k2 * k2)
    ssum = _dot(zhi, seg_ref[...]) + _dot(zlo, seg_ref[...])
    rs = lax.rsqrt(ssum * (1.0 / HEAD_DIM) + EPS)
    n1 = k1 * rs * knw_ref[:, :LANES]
    n2 = k2 * rs * knw_ref[:, LANES:]
    kr = jnp.concatenate([n1 * cos - n2 * sin, n2 * cos + n1 * sin], axis=1)
    kt_ref[0] = kr.T.astype(BF16)
    v_ref[0] = _dot(u, wv_ref[...]).astype(BF16)


def _shared_kv(h, posf, nw, wk, wv, knw, invf, seg):
    bsz, seq, _ = h.shape
    consts = (nw, wk, wv, knw, invf, seg)
    return pl.pallas_call(
        _kv_kernel,
        grid=(bsz, seq // TS),
        in_specs=[pl.BlockSpec((1, TS, D_MODEL), lambda b, t: (b, t, 0)),
                  pl.BlockSpec((1, TS, 1), lambda b, t: (b, t, 0))]
                 + [_const_spec(w.shape) for w in consts],
        out_specs=[pl.BlockSpec((1, KV_WIDTH, TS), lambda b, t: (b, 0, t)),
                   pl.BlockSpec((1, TS, 2 * KV_WIDTH), lambda b, t: (b, t, 0)),
                   pl.BlockSpec((1, TS, LANES), lambda b, t: (b, t, 0)),
                   pl.BlockSpec((1, TS, LANES), lambda b, t: (b, t, 0))],
        out_shape=[jax.ShapeDtypeStruct((bsz, KV_WIDTH, seq), BF16),
                   jax.ShapeDtypeStruct((bsz, seq, 2 * KV_WIDTH), BF16),
                   jax.ShapeDtypeStruct((bsz, seq, LANES), F32),
                   jax.ShapeDtypeStruct((bsz, seq, LANES), F32)],
        compiler_params=pltpu.CompilerParams(dimension_semantics=("arbitrary", "arbitrary"),
                                             vmem_limit_bytes=VMEM_LIMIT),
        name="shared_kv",
    )(h, posf, *consts)


_ATTN_REF_COUNTS = (None,) * 10 + (ATT_WIDTH // WCOL, ATT_WIDTH // WCOL, None, None, D_MODEL // WCOL, None,
                                   D_MODEL // WCOL, D_MODEL // WCOL) + (None,) * 7


def _attn_kernel(*refs):
    (sink_ref, x_ref, p_ref, cos_ref, sin_ref, ktc_ref, ktp_ref, vc_ref, vp_ref, nw_ref, wq_ref, wg_ref, qnw_ref,
     seg_ref, ow_ref, pnw_ref, pgw_ref, ppw_ref, o_ref, u_s, q16_s, krhs_s, gs_s, o_s, h2_s) = _take(refs, _ATTN_REF_COUNTS)
    t = pl.program_id(1)
    qw = ATT_WIDTH // 2

    @pl.when(t == 0)
    def _():
        krhs_s[...] = jnp.zeros(krhs_s.shape, BF16)

    u_s[...] = _rms(x_ref[0], nw_ref[...]).astype(BF16)

    ii = lax.broadcasted_iota(jnp.int32, (BLOCK, BLOCK), 0)
    jj = lax.broadcasted_iota(jnp.int32, (BLOCK, BLOCK), 1)
    own_ok = jj <= ii
    first_off = jnp.where(t > 0, 0, BLOCK)
    valid_first = jnp.concatenate([jj > ii + first_off, own_ok], axis=1)
    valid_rest = jnp.concatenate([jj > ii, own_ok], axis=1)
    lo_half = lax.broadcasted_iota(jnp.int32, (2 * BLOCK, LANES), 1) < HEAD_DIM
    scale = HEAD_DIM ** -0.5

    def q_item(b):
        r0 = b * BLOCK
        u = u_s[r0:r0 + BLOCK, :]
        q1 = _dot(u, wq_ref[0][...])
        q2 = _dot(u, wq_ref[1][...])
        zhi, zlo = _split2(q1 * q1 + q2 * q2)
        sw = seg_ref.shape[0]
        ssum = jnp.concatenate([_dot(zhi[:, c:c + sw], seg_ref[...]) + _dot(zlo[:, c:c + sw], seg_ref[...])
                                for c in range(0, qw, sw)], axis=1)
        rs = lax.rsqrt(ssum * (1.0 / HEAD_DIM) + EPS)
        n1 = q1 * rs * qnw_ref[:, :qw]
        n2 = q2 * rs * qnw_ref[:, qw:]
        cos = jnp.concatenate([cos_ref[0, r0:r0 + BLOCK, :]] * (qw // LANES), axis=1)
        sin = jnp.concatenate([sin_ref[0, r0:r0 + BLOCK, :]] * (qw // LANES), axis=1)
        q16_s[r0:r0 + BLOCK, :qw] = ((n1 * cos - n2 * sin) * scale).astype(BF16)
        q16_s[r0:r0 + BLOCK, qw:] = ((n2 * cos + n1 * sin) * scale).astype(BF16)

    def gate_item(b, k):
        r0 = b * BLOCK
        gs_s[r0:r0 + BLOCK, k * WCOL:(k + 1) * WCOL] = _silu(_dot(u_s[r0:r0 + BLOCK, :], wg_ref[k][...]))

    def kv_blocks(b):
        r0 = b * BLOCK
        if b == 0:
            return ktp_ref[0], ktc_ref[0, :, 0:BLOCK], vp_ref[0], vc_ref[0, 0:BLOCK, :]
        return (ktc_ref[0, :, r0 - BLOCK:r0], ktc_ref[0, :, r0:r0 + BLOCK],
                vc_ref[0, r0 - BLOCK:r0, :], vc_ref[0, r0:r0 + BLOCK, :])

    scores = {}

    def score_item(b, g):
        r0 = b * BLOCK
        slot = (b * KV_HEADS + g) % 2
        kprev, kown, _, _ = kv_blocks(b)
        for hl in range(Q_PER_KV):
            for half in range(2):
                src = half * LANES + g * HALF
                dst = half * LANES + hl * HALF
                krhs_s[slot, dst:dst + HALF, hl * 2 * BLOCK:hl * 2 * BLOCK + BLOCK] = kprev[src:src + HALF, :]
                krhs_s[slot, dst:dst + HALF, hl * 2 * BLOCK + BLOCK:(hl + 1) * 2 * BLOCK] = kown[src:src + HALF, :]
        lhs = jnp.concatenate([q16_s[r0:r0 + BLOCK, g * LANES:(g + 1) * LANES],
                               q16_s[r0:r0 + BLOCK, qw + g * LANES:qw + (g + 1) * LANES]], axis=1)
        scores[(b, g)] = _dot(lhs, krhs_s[slot])

    def softmax_pv_item(b, g):
        r0 = b * BLOCK
        s_all = scores.pop((b, g))
        valid = valid_first if b == 0 else valid_rest
        _, _, vprev, vown = kv_blocks(b)
        vg = jnp.concatenate([vprev[:, g * LANES:(g + 1) * LANES], vown[:, g * LANES:(g + 1) * LANES]], axis=0)
        zero = jnp.zeros_like(vg)
        vbd = jnp.concatenate([jnp.where(lo_half, vg, zero), jnp.where(lo_half, zero, vg)], axis=0)
        probs = []
        for hl in range(Q_PER_KV):
            sink = sink_ref[g * Q_PER_KV + hl]
            s = jnp.where(valid, s_all[:, hl * 2 * BLOCK:(hl + 1) * 2 * BLOCK], -jnp.inf)
            m = jnp.maximum(jnp.max(s, axis=-1, keepdims=True), sink)
            e = jnp.exp(s - m)
            den = jnp.sum(e, axis=-1, keepdims=True) + jnp.exp(sink - m)
            probs.append((e * (1.0 / den)).astype(BF16))
        for pr in range(Q_PER_KV // 2):
            lhs_p = jnp.concatenate([probs[2 * pr], probs[2 * pr + 1]], axis=1)
            c0 = (g * 2 + pr) * LANES
            o_s[r0:r0 + BLOCK, c0:c0 + LANES] = _dot(lhs_p, vbd)

    def out_item(b):
        r0 = b * BLOCK
        og = (o_s[r0:r0 + BLOCK, :] * gs_s[r0:r0 + BLOCK, :]).astype(BF16)
        h2_s[r0:r0 + BLOCK, :] = x_ref[0, r0:r0 + BLOCK, :] + _dotw(og, ow_ref)

    def ple_item(b):
        r0 = b * BLOCK
        o_ref[0, r0:r0 + BLOCK, :] = _ple(h2_s[r0:r0 + BLOCK, :], p_ref[0, r0:r0 + BLOCK, :],
                                          pnw_ref, pgw_ref, ppw_ref)

    q_item(0)
    score_item(0, 0)
    for b in range(ANB):
        more = b + 1 < ANB
        for g in range(KV_HEADS):
            if g + 1 < KV_HEADS:
                score_item(b, g + 1)
            elif more:
                score_item(b + 1, 0)
            if g == 0:
                gate_item(b, 0)
                gate_item(b, 1)
            elif g == 1 and b > 0:
                out_item(b - 1)
            elif g == 2 and more:
                q_item(b + 1)
            elif g == 3 and b > 0:
                ple_item(b - 1)
            softmax_pv_item(b, g)
    out_item(ANB - 1)
    ple_item(ANB - 1)


def _attn_layer(h, p, layer, j, sinks, cos, sin, kt, v2, q_w16, in_w16, out_w16, gate_w16, proj_w16, nw, qnw, seg, pnw):
    bsz, seq, _ = h.shape
    n_half = ATT_WIDTH // WCOL
    wq, wq_specs = _chunks(q_w16, j, 0, n_half)
    wg, wg_specs = _chunks(in_w16, j, n_half, n_half)
    ow, ow_specs = _chunks(out_w16, j, 0, D_MODEL // WCOL)
    pgw, pgw_specs = _chunks(gate_w16, layer, 0, D_MODEL // WCOL)
    ppw, ppw_specs = _chunks(proj_w16, layer, 0, D_MODEL // WCOL)
    small = lambda *ws: [_const_spec(w.shape) for w in ws]
    consts = (nw, *wq, *wg, qnw, seg, *ow, pnw, *pgw, *ppw)
    const_specs = small(nw) + wq_specs + wg_specs + small(qnw, seg) + ow_specs + small(pnw) + pgw_specs + ppw_specs
    prev_blk = lambda t: jnp.maximum(t * ANB - 1, 0)
    return pl.pallas_call(
        _attn_kernel,
        grid=(bsz, seq // ATS),
        in_specs=[pl.BlockSpec(memory_space=pltpu.SMEM),
                  pl.BlockSpec((1, ATS, D_MODEL), lambda b, t: (b, t, 0)),
                  _layer_p_spec(layer, ATS),
                  pl.BlockSpec((1, ATS, LANES), lambda b, t: (b, t, 0)),
                  pl.BlockSpec((1, ATS, LANES), lambda b, t: (b, t, 0)),
                  pl.BlockSpec((1, KV_WIDTH, ATS), lambda b, t: (b, 0, t)),
                  pl.BlockSpec((1, KV_WIDTH, BLOCK), lambda b, t: (b, 0, prev_blk(t))),
                  pl.BlockSpec((1, ATS, 2 * KV_WIDTH), lambda b, t: (b, t, 0)),
                  pl.BlockSpec((1, BLOCK, 2 * KV_WIDTH), lambda b, t: (b, prev_blk(t), 0))]
                 + const_specs,
        out_specs=pl.BlockSpec((1, ATS, D_MODEL), lambda b, t: (b, t, 0)),
        out_shape=jax.ShapeDtypeStruct(h.shape, F32),
        scratch_shapes=[pltpu.VMEM((ATS, D_MODEL), BF16),
                        pltpu.VMEM((ATS, ATT_WIDTH), BF16),
                        pltpu.VMEM((2, 2 * LANES, Q_PER_KV * 2 * BLOCK), BF16),
                        pltpu.VMEM((ATS, ATT_WIDTH), F32),
                        pltpu.VMEM((ATS, ATT_WIDTH), F32),
                        pltpu.VMEM((ATS, D_MODEL), F32)],
        compiler_params=pltpu.CompilerParams(dimension_semantics=("arbitrary", "arbitrary"),
                                             vmem_limit_bytes=VMEM_LIMIT),
        name="attn_layer",
    )(sinks, h, p, cos, sin, kt, kt, v2, v2, *consts)


def _row(v):
    return v.reshape(1, -1).astype(F32)


def _pad_lanes(v):
    return jnp.pad(v, [(0, 0)] * (v.ndim - 1) + [(0, LANES - v.shape[-1])])


def _halves_perm(n_heads):
    d = np.arange(HALF)
    first = (np.arange(n_heads)[:, None] * HEAD_DIM + d[None, :]).reshape(-1)
    return np.concatenate([first, first + HALF])


def _segment_ones(width):
    idx = np.arange(width) // HALF
    return jnp.asarray(idx[:, None] == idx[None, :], dtype=BF16)


def kernel(x, p, positions, ssm_norm_w, ssm_in_w, ssm_conv_w, ssm_conv_b, ssm_dt_bias, ssm_a_log, ssm_d,
           ssm_gnorm_w, ssm_out_w, kv_norm_w, kv_w, k_norm_w, attn_norm_w, attn_in_w, q_norm_w, attn_sinks,
           attn_out_w, ple_norm_w, ple_gate_w, ple_proj_w):
    n_a = ssm_in_w.shape[0]
    n_b = attn_in_w.shape[0]
    head_order = np.concatenate([np.arange(0, SSM_HEADS, 2), np.arange(1, SSM_HEADS, 2)])
    ssm_in_w16 = ssm_in_w.astype(BF16)
    ssm_out_w16 = ssm_out_w.astype(BF16)
    gate_w16 = ple_gate_w.astype(BF16)
    proj_w16 = ple_proj_w.astype(BF16)
    h = x
    for i in range(n_a):
        h = _ssm_layer(
            h, p, i, ssm_in_w16, ssm_out_w16, gate_w16, proj_w16, _row(ssm_norm_w[i]),
            _pad_lanes(ssm_in_w[i][:, D_INNER + CONV_CH:][:, head_order]).astype(BF16),
            ssm_conv_w[i].astype(F32), _row(ssm_conv_b[i]),
            _pad_lanes(_row(ssm_dt_bias[i][head_order])), _pad_lanes(_row(ssm_a_log[i][head_order])),
            _row(jnp.repeat(ssm_d[i], SSM_HEAD_DIM)), _row(ssm_gnorm_w[i]), _row(ple_norm_w[i]))

    inv_freq = ROPE_THETA ** (-(jnp.arange(HALF, dtype=F32) * 2.0 / HEAD_DIM))
    kperm = _halves_perm(KV_HEADS)
    qperm = _halves_perm(ATT_HEADS)
    posf = positions.astype(F32)[..., None]
    wv = kv_w[:, KV_WIDTH:].reshape(D_MODEL, KV_HEADS, 1, HEAD_DIM)
    wv = jnp.broadcast_to(wv, (D_MODEL, KV_HEADS, 2, HEAD_DIM)).reshape(D_MODEL, 2 * KV_WIDTH)
    kt, v2, cos, sin = _shared_kv(
        h, posf, _row(kv_norm_w), kv_w[:, :KV_WIDTH][:, kperm].astype(BF16), wv.astype(BF16),
        _row(jnp.tile(k_norm_w.reshape(2, HALF), (1, KV_HEADS))),
        _row(jnp.tile(inv_freq, LANES // HALF)), _segment_ones(LANES))

    seg_q = _segment_ones(2 * LANES)
    attn_in_w16 = attn_in_w.astype(BF16)
    attn_q_w16 = attn_in_w[:, :, :ATT_WIDTH][:, :, qperm].astype(BF16)
    attn_out_w16 = attn_out_w.astype(BF16)
    for j in range(n_b):
        i = n_a + j
        h = _attn_layer(
            h, p, i, j, attn_sinks[j].astype(F32), cos, sin, kt, v2,
            attn_q_w16, attn_in_w16, attn_out_w16, gate_w16, proj_w16, _row(attn_norm_w[j]),
            _row(jnp.tile(q_norm_w[j].reshape(2, HALF), (1, ATT_HEADS))), seg_q, _row(ple_norm_w[i]))
    return h
```

```python
import functools

import jax
import jax.numpy as jnp
import numpy as np
from jax import lax
from jax.experimental import pallas as pl
from jax.experimental.pallas import tpu as pltpu

F32 = jnp.float32
BF16 = jnp.bfloat16

D_MODEL = 1024
D_INNER = 2048
SSM_HEAD_DIM = 64
SSM_HEADS = 32
SSM_GROUPS = 4
SSM_HPG = 8
D_STATE = 128
CONV_WIDTH = 4
BC_WIDTH = SSM_GROUPS * D_STATE
CONV_CH = D_INNER + 2 * BC_WIDTH
CHUNK = 128
ATT_HEADS = 16
KV_HEADS = 4
HEAD_DIM = 64
HALF = HEAD_DIM // 2
Q_PER_KV = 4
ATT_WIDTH = 1024
KV_WIDTH = 256
BLOCK = 128
ROPE_THETA = 10000.0
PLE_DIM = 256
EPS = 1e-6

LANES = 128
SUBLANES = 8
TS = 512
ATS = 512
ANB = ATS // BLOCK
STS = 512
SNCH = STS // CHUNK
MROWS = 256
VREGS_PER_CHUNK = CHUNK // SUBLANES
CTAIL = (CONV_WIDTH - 1) * SUBLANES
WCOL = 512
GW = D_INNER // SSM_GROUPS
PAIRS = SSM_HEADS // 2
PPG = SSM_HPG // 2
VMEM_LIMIT = 56 * 1024 * 1024

_NT = (((1,), (1,)), ((), ()))


def _dot(a, b):
    return jnp.dot(a, b, preferred_element_type=F32)


def _dotw(a, w_refs):
    return jnp.concatenate([_dot(a, w[...]) for w in w_refs], axis=1)


def _rms(x, w):
    ms = jnp.mean(x * x, axis=-1, keepdims=True)
    return x * lax.rsqrt(ms + EPS) * w


def _sigmoid(x):
    return 0.5 * jnp.tanh(0.5 * x) + 0.5


def _silu(x):
    hx = 0.5 * x
    return hx * jnp.tanh(hx) + hx


def _split2(x):
    hi = x.astype(BF16)
    lo = (x - hi.astype(F32)).astype(BF16)
    return hi, lo


def _ple_apply(h2, u2, p_tile, pgw_ref, ppw_ref):
    gate = _sigmoid(_dotw(u2, pgw_ref))
    proj = _dotw(p_tile.astype(BF16), ppw_ref)
    return h2 + gate * proj


def _ple(h2, p_tile, pnw_ref, pgw_ref, ppw_ref):
    return _ple_apply(h2, _rms(h2, pnw_ref[...]).astype(BF16), p_tile, pgw_ref, ppw_ref)


def _time_of_row(r):
    return (r & (SUBLANES - 1)) * VREGS_PER_CHUNK + (r >> 3)


def _scatter_rows(ref, tile0, row0, val, to_kernel_order):
    for tile in range(val.shape[1] // LANES):
        for i in range(val.shape[0] // SUBLANES):
            chunk, v = divmod(i, VREGS_PER_CHUNK)
            if to_kernel_order:
                rows = pl.ds(row0 + chunk * CHUNK + (CHUNK // 2) * (v % 2) + v // 2, SUBLANES, stride=SUBLANES)
            else:
                rows = pl.ds(row0 + chunk * CHUNK + v, SUBLANES, stride=VREGS_PER_CHUNK)
            piece = val[i * SUBLANES:(i + 1) * SUBLANES, tile * LANES:(tile + 1) * LANES]
            if len(ref.shape) == 3:
                ref[pl.ds(tile0 + tile, 1), rows, :] = piece[None]
            else:
                ref[rows, :] = piece


def _gather_tiles(ref, tile0, n_tiles, rows):
    return jnp.concatenate([ref[tile0 + k, rows, :] for k in range(n_tiles)], axis=1)


def _take(refs, counts):
    out, i = [], 0
    for n in counts:
        out.append(refs[i] if n is None else refs[i:i + n])
        i += 1 if n is None else n
    assert i == len(refs)
    return out


_SSM_REF_COUNTS = (None, None, None, None, D_INNER // WCOL, CONV_CH // WCOL, None, None, None, None, None, None, None,
                   D_MODEL // WCOL, None, D_MODEL // WCOL, D_MODEL // WCOL) + (None,) * 9


def _ssm_kernel(*refs, tiles_per_row, n_tiles):
    (x_ref, xp_ref, pp_ref, nw_ref, wz_ref, wxbc_ref, wdt_ref, cw_ref, cb_ref, dtb_ref, alog_ref, dsk_ref, gnw_ref,
     ow_ref, pnw_ref, pgw_ref, ppw_ref, o_ref, cbuf, u_s, dt_s, zs_s, y16_s, y16p_s, h2_s, st_s) = _take(
        refs, _SSM_REF_COUNTS)
    s = pl.program_id(0)
    row_start = s % tiles_per_row == 0

    @pl.when(s == 0)
    def _():
        y16p_s[...] = jnp.zeros(y16p_s.shape, BF16)

    @pl.when(row_start)
    def _():
        cbuf[:, 0:CTAIL, :] = jnp.zeros((CONV_CH // LANES, CTAIL, LANES), F32)
        st_s[...] = jnp.zeros(st_s.shape, F32)

    @pl.when(jnp.logical_not(row_start))
    def _():
        cbuf[:, 0:CTAIL, :] = cbuf[:, STS:STS + CTAIL, :]

    ii = lax.broadcasted_iota(jnp.int32, (CHUNK, LANES), 0)
    ll = lax.broadcasted_iota(jnp.int32, (CHUNK, LANES), 1)
    lo_half = ll < SSM_HEAD_DIM
    time_i = _time_of_row(ii)
    time_lo = _time_of_row(jnp.where(lo_half, ll, ll - SSM_HEAD_DIM))
    mask_lo = time_lo <= time_i
    mask_hi = time_lo + SUBLANES <= time_i
    ltri = (_time_of_row(ll) <= time_i).astype(F32)
    lo_row = lax.broadcasted_iota(jnp.int32, (PAIRS, LANES), 1) < SSM_HEAD_DIM
    lo_k = lax.broadcasted_iota(jnp.int32, (SSM_HEAD_DIM, LANES), 1) < SSM_HEAD_DIM

    tabs = []

    def norm_dt_item():
        u_s[...] = _rms(x_ref[0], nw_ref[...]).astype(BF16)
        _scatter_rows(dt_s, 0, 0, _dot(u_s[...], wdt_ref[...]) + dtb_ref[...], True)
        dt_raw = dt_s[...]
        dt = jnp.maximum(dt_raw, 0.0) + jnp.log1p(jnp.exp(-jnp.abs(dt_raw)))
        d_a = dt * (-jnp.exp(alog_ref[...]))
        for c in range(SNCH):
            r0 = c * CHUNK
            dt_c = dt[r0:r0 + CHUNK]
            acum = jnp.dot(ltri, d_a[r0:r0 + CHUNK], precision=lax.Precision.HIGHEST,
                           preferred_element_type=F32)
            acum_t = acum.T
            dt_t = dt_c.T
            ev, od = acum_t[:PAIRS], acum_t[PAIRS:2 * PAIRS]
            dev, dod = dt_t[:PAIRS], dt_t[PAIRS:2 * PAIRS]
            tabs.append(dict(
                acum=acum,
                w_out=jnp.exp(acum[CHUNK - 1:CHUNK, :] - acum) * dt_c,
                a_lo=jnp.where(lo_row, ev, pltpu.roll(od, SSM_HEAD_DIM, 1)),
                a_hi=jnp.where(lo_row, pltpu.roll(ev, SSM_HEAD_DIM, 1), od),
                d_lo=jnp.where(lo_row, dev, pltpu.roll(dod, SSM_HEAD_DIM, 1)),
                d_hi=jnp.where(lo_row, pltpu.roll(dev, SSM_HEAD_DIM, 1), dod)))

    def pair_expand(v, pair):
        idx = jnp.where(lo_half, pair, PAIRS + pair)
        return jnp.take_along_axis(v, idx, axis=1)

    tiles_per_wcol = WCOL // LANES

    def proj_item(hf, j):
        r0 = hf * MROWS
        _scatter_rows(cbuf, j * tiles_per_wcol, CTAIL + r0, _dot(u_s[r0:r0 + MROWS, :], wxbc_ref[j][...]), True)

    def z_item(hf, g):
        r0 = hf * MROWS
        _scatter_rows(zs_s, g * tiles_per_wcol, r0, _silu(_dot(u_s[r0:r0 + MROWS, :], wz_ref[g][...])), True)

    def conv(r0, c0, width):
        base = CTAIL + r0
        tile0, n_tiles = c0 // LANES, width // LANES
        x = _gather_tiles(cbuf, tile0, n_tiles, slice(base, base + CHUNK))
        last = lax.broadcasted_iota(jnp.int32, (SUBLANES, width), 0) == SUBLANES - 1
        wrapped = []
        for v in range(VREGS_PER_CHUNK - CONV_WIDTH + 1, VREGS_PER_CHUNK):
            prev = _gather_tiles(cbuf, tile0, n_tiles,
                                 slice(base - CHUNK + v * SUBLANES, base - CHUNK + (v + 1) * SUBLANES))
            wrapped.append(pltpu.roll(jnp.where(last, prev, x[v * SUBLANES:(v + 1) * SUBLANES]), 1, 0))
        acc = cb_ref[:, c0:c0 + width] + cw_ref[CONV_WIDTH - 1:CONV_WIDTH, c0:c0 + width] * x
        for k in range(1, CONV_WIDTH):
            shifted = jnp.concatenate(wrapped[CONV_WIDTH - 1 - k:] + [x[:CHUNK - k * SUBLANES]], axis=0)
            acc = acc + cw_ref[CONV_WIDTH - 1 - k:CONV_WIDTH - k, c0:c0 + width] * shifted
        return _silu(acc)

    prepared = {}

    def ssd_prepare(c, g):
        r0, gc, tab = c * CHUNK, g * GW, tabs[c]
        xs = conv(r0, gc, GW)
        bm = conv(r0, D_INNER + g * D_STATE, D_STATE)
        cm = conv(r0, D_INNER + BC_WIDTH + g * D_STATE, D_STATE)
        cm16, bm16 = cm.astype(BF16), bm.astype(BF16)
        h0, h1 = bm16[:SSM_HEAD_DIM], bm16[SSM_HEAD_DIM:]
        cb2 = lax.dot_general(cm16, jnp.concatenate([h0, h0, h1, h1], axis=0), _NT,
                              preferred_element_type=F32)
        cb_lo, cb_hi = cb2[:, :LANES], cb2[:, LANES:]
        a_exp = [pair_expand(tab["acum"], g * PPG + q) for q in range(PPG)]
        w_exp = jnp.concatenate([pair_expand(tab["w_out"], g * PPG + q) for q in range(PPG)], axis=1)
        a_all = jnp.concatenate(a_exp, axis=1)
        st = st_s[g]
        y_off = _dot(cm16, st.astype(BF16))
        st_s[g] = st * jnp.exp(a_all[CHUNK - 1:CHUNK, :]) + _dot(bm.T.astype(BF16), (xs * w_exp).astype(BF16))

        operands = []
        for q in range(PPG):
            pair = g * PPG + q
            t_lo = cb_lo * jnp.exp(jnp.where(mask_lo, a_exp[q] - tab["a_lo"][pair:pair + 1, :], -jnp.inf))
            t_hi = cb_hi * jnp.exp(jnp.where(mask_hi, a_exp[q] - tab["a_hi"][pair:pair + 1, :], -jnp.inf))
            lhs = jnp.concatenate([(t_lo * tab["d_lo"][pair:pair + 1, :]).astype(BF16),
                                   (t_hi * tab["d_hi"][pair:pair + 1, :]).astype(BF16)], axis=1)
            xs16 = xs[:, q * LANES:(q + 1) * LANES].astype(BF16)
            x0, x1 = xs16[:SSM_HEAD_DIM], xs16[SSM_HEAD_DIM:]
            zero = jnp.zeros_like(x0)
            rhs = jnp.concatenate([jnp.where(lo_k, x0, zero), jnp.where(lo_k, zero, x0),
                                   jnp.where(lo_k, x1, zero), jnp.where(lo_k, zero, x1)], axis=0)
            operands.append((lhs, rhs))
        prepared[(c, g)] = operands, jnp.exp(a_all) * y_off + dsk_ref[:, gc:gc + GW] * xs

    def ssd_finish(c, g):
        r0, gc = c * CHUNK, g * GW
        operands, rest = prepared.pop((c, g))
        y = jnp.concatenate([_dot(lhs, rhs) for lhs, rhs in operands], axis=1) + rest
        y = y * _gather_tiles(zs_s, gc // LANES, GW // LANES, slice(r0, r0 + CHUNK))
        y = y * lax.rsqrt(jnp.mean(y * y, axis=-1, keepdims=True) + EPS) * gnw_ref[:, gc:gc + GW]
        y16_s[r0:r0 + CHUNK, gc:gc + GW] = y.astype(BF16)

    def out_item(hf, k):
        r0 = hf * MROWS
        _scatter_rows(h2_s, k * tiles_per_wcol, r0, _dot(y16p_s[r0:r0 + MROWS, :], ow_ref[k][...]), False)

    normed = {}

    def ple_norm_item(hf):
        r0 = hf * MROWS
        h2 = xp_ref[0, r0:r0 + MROWS, :] + _gather_tiles(h2_s, 0, D_MODEL // LANES, slice(r0, r0 + MROWS))
        normed[hf] = h2, _rms(h2, pnw_ref[...]).astype(BF16)

    def ple_item(hf):
        r0 = hf * MROWS
        h2, u2 = normed.pop(hf)
        o_ref[0, r0:r0 + MROWS, :] = _ple_apply(h2, u2, pp_ref[0, r0:r0 + MROWS, :], pgw_ref, ppw_ref)

    def keep_item():
        y16p_s[...] = y16_s[...]

    n_, p_, z_, a_, f_, o_, e_, g_, k_ = (norm_dt_item, proj_item, z_item, ssd_prepare, ssd_finish, out_item,
                                          ple_norm_item, ple_item, keep_item)
    bc = (D_INNER // WCOL, D_INNER // WCOL + 1)
    order = [(n_,), (p_, 0, bc[0]), (p_, 0, bc[1]), (p_, 0, 0),
             (a_, 0, 0), (z_, 0, 0), (f_, 0, 0), (p_, 0, 1),
             (a_, 0, 1), (z_, 0, 1), (f_, 0, 1), (p_, 0, 2),
             (a_, 0, 2), (z_, 0, 2), (f_, 0, 2), (p_, 0, 3),
             (a_, 0, 3), (z_, 0, 3), (f_, 0, 3), (p_, 1, bc[0]),
             (a_, 1, 0), (p_, 1, bc[1]), (f_, 1, 0), (p_, 1, 0),
             (a_, 1, 1), (z_, 1, 0), (f_, 1, 1), (p_, 1, 1),
             (a_, 1, 2), (z_, 1, 1), (f_, 1, 2), (p_, 1, 2),
             (a_, 1, 3), (z_, 1, 2), (f_, 1, 3), (p_, 1, 3),
             (a_, 2, 0), (z_, 1, 3), (f_, 2, 0), (o_, 0, 0),
             (a_, 2, 1), (o_, 0, 1), (f_, 2, 1), (e_, 0),
             (a_, 2, 2), (f_, 2, 2), (g_, 0),
             (a_, 2, 3), (o_, 1, 0), (f_, 2, 3),
             (a_, 3, 0), (f_, 3, 0), (o_, 1, 1),
             (a_, 3, 1), (e_, 1), (f_, 3, 1),
             (a_, 3, 2), (g_, 1), (f_, 3, 2),
             (a_, 3, 3), (f_, 3, 3), (k_,)]
    flush = [(o_, 0, 0), (o_, 0, 1), (e_, 0), (o_, 1, 0), (o_, 1, 1), (g_, 0), (e_, 1), (g_, 1)]
    assert SNCH == 4 and STS == 2 * MROWS

    @pl.when(s < n_tiles)
    def _():
        for item, *args in order:
            item(*args)

    @pl.when(s == n_tiles)
    def _():
        for item, *args in flush:
            item(*args)


def _const_spec(shape):
    nd = len(shape)
    return pl.BlockSpec(shape, lambda *_: (0,) * nd, pipeline_mode=pl.Buffered(1))


def _chunk_spec(layer, col_block, rows):
    return pl.BlockSpec((None, rows, WCOL), lambda *_: (layer, 0, col_block), pipeline_mode=pl.Buffered(1))


def _chunks(w_all, layer, first_block, count):
    specs = [_chunk_spec(layer, first_block + c, w_all.shape[1]) for c in range(count)]
    return [w_all] * count, specs


def _ssm_layer(h, p, layer, in_w16, out_w16, gate_w16, proj_w16, nw, wdt, cw, cb, dtb, alog, dsk, gnw, pnw):
    bsz, seq, _ = h.shape
    wz, wz_specs = _chunks(in_w16, layer, 0, D_INNER // WCOL)
    wxbc, wxbc_specs = _chunks(in_w16, layer, D_INNER // WCOL, CONV_CH // WCOL)
    ow, ow_specs = _chunks(out_w16, layer, 0, D_MODEL // WCOL)
    pgw, pgw_specs = _chunks(gate_w16, layer, 0, D_MODEL // WCOL)
    ppw, ppw_specs = _chunks(proj_w16, layer, 0, D_MODEL // WCOL)
    small = lambda *ws: [_const_spec(w.shape) for w in ws]
    weights = (nw, *wz, *wxbc, wdt, cw, cb, dtb, alog, dsk, gnw, *ow, pnw, *pgw, *ppw)
    weight_specs = (small(nw) + wz_specs + wxbc_specs + small(wdt, cw, cb, dtb, alog, dsk, gnw) + ow_specs
                    + small(pnw) + pgw_specs + ppw_specs)
    nt = seq // STS
    n_tiles = bsz * nt

    def this_tile(s):
        n = jnp.minimum(s, n_tiles - 1)
        return n // nt, n % nt

    def prev_tile(s):
        n = jnp.maximum(s - 1, 0)
        return n // nt, n % nt

    return pl.pallas_call(
        functools.partial(_ssm_kernel, tiles_per_row=nt, n_tiles=n_tiles),
        grid=(n_tiles + 1,),
        in_specs=[pl.BlockSpec((1, STS, D_MODEL), lambda s: (*this_tile(s), 0)),
                  pl.BlockSpec((1, STS, D_MODEL), lambda s: (*prev_tile(s), 0)),
                  pl.BlockSpec((None, 1, STS, PLE_DIM), lambda s: (layer, *prev_tile(s), 0))]
                 + weight_specs,
        out_specs=pl.BlockSpec((1, STS, D_MODEL), lambda s: (*prev_tile(s), 0)),
        out_shape=jax.ShapeDtypeStruct(h.shape, F32),
        scratch_shapes=[pltpu.VMEM((CONV_CH // LANES, CTAIL + STS, LANES), F32),
                        pltpu.VMEM((STS, D_MODEL), BF16),
                        pltpu.VMEM((STS, LANES), F32),
                        pltpu.VMEM((D_INNER // LANES, STS, LANES), F32),
                        pltpu.VMEM((STS, D_INNER), BF16),
                        pltpu.VMEM((STS, D_INNER), BF16),
                        pltpu.VMEM((D_MODEL // LANES, STS, LANES), F32),
                        pltpu.VMEM((SSM_GROUPS, D_STATE, GW), F32)],
        compiler_params=pltpu.CompilerParams(dimension_semantics=("arbitrary",),
                                             vmem_limit_bytes=VMEM_LIMIT),
        name="ssm_layer",
    )(h, h, p, *weights)


def _kv_kernel(x_ref, pos_ref, nw_ref, wk_ref, wv_ref, knw_ref, invf_ref, seg_ref,
               kt_ref, v_ref, cos_ref, sin_ref):
    u = _rms(x_ref[0], nw_ref[...]).astype(BF16)
    ang = pos_ref[0] * invf_ref[...]
    cos = jnp.cos(ang)
    sin = jnp.sin(ang)
    cos_ref[0] = cos
    sin_ref[0] = sin
    kk = _dot(u, wk_ref[...])
    k1, k2 = kk[:, :LANES], kk[:, LANES:]
    zhi, zlo = _split2(k1 * k1 + k2 * k2)
    ssum = _dot(zhi, seg_ref[...]) + _dot(zlo, seg_ref[...])
    rs = lax.rsqrt(ssum * (1.0 / HEAD_DIM) + EPS)
    n1 = k1 * rs * knw_ref[:, :LANES]
    n2 = k2 * rs * knw_ref[:, LANES:]
    kr = jnp.concatenate([n1 * cos - n2 * sin, n2 * cos + n1 * sin], axis=1)
    kt_ref[0] = kr.T.astype(BF16)
    v_ref[0] = _dot(u, wv_ref[...]).astype(BF16)


def _shared_kv(h, posf, nw, wk, wv, knw, invf, seg):
    bsz, seq, _ = h.shape
    consts = (nw, wk, wv, knw, invf, seg)
    return pl.pallas_call(
        _kv_kernel,
        grid=(bsz, seq // TS),
        in_specs=[pl.BlockSpec((1, TS, D_MODEL), lambda b, t: (b, t, 0)),
                  pl.BlockSpec((1, TS, 1), lambda b, t: (b, t, 0))]
                 + [_const_spec(w.shape) for w in consts],
        out_specs=[pl.BlockSpec((1, KV_WIDTH, TS), lambda b, t: (b, 0, t)),
                   pl.BlockSpec((1, TS, 2 * KV_WIDTH), lambda b, t: (b, t, 0)),
                   pl.BlockSpec((1, TS, LANES), lambda b, t: (b, t, 0)),
                   pl.BlockSpec((1, TS, LANES), lambda b, t: (b, t, 0))],
        out_shape=[jax.ShapeDtypeStruct((bsz, KV_WIDTH, seq), BF16),
                   jax.ShapeDtypeStruct((bsz, seq, 2 * KV_WIDTH), BF16),
                   jax.ShapeDtypeStruct((bsz, seq, LANES), F32),
                   jax.ShapeDtypeStruct((bsz, seq, LANES), F32)],
        compiler_params=pltpu.CompilerParams(dimension_semantics=("arbitrary", "arbitrary"),
                                             vmem_limit_bytes=VMEM_LIMIT),
        name="shared_kv",
    )(h, posf, *consts)


_ATTN_REF_COUNTS = (None,) * 11 + (ATT_WIDTH // WCOL, ATT_WIDTH // WCOL, None, None, D_MODEL // WCOL, None,
                                   D_MODEL // WCOL, D_MODEL // WCOL) + (None,) * 9


def _attn_kernel(*refs, tiles_per_row, n_tiles):
    (sink_ref, x_ref, xp_ref, pp_ref, cos_ref, sin_ref, ktc_ref, ktp_ref, vc_ref, vp_ref, nw_ref, wq_ref, wg_ref,
     qnw_ref, seg_ref, ow_ref, pnw_ref, pgw_ref, ppw_ref, o_ref, u_s, q16_s, krhs_s, gs_s, o_s, og16_s, og16p_s,
     h2_s) = _take(refs, _ATTN_REF_COUNTS)
    s = pl.program_id(0)
    qw = ATT_WIDTH // 2

    @pl.when(s == 0)
    def _():
        krhs_s[...] = jnp.zeros(krhs_s.shape, BF16)
        og16p_s[...] = jnp.zeros(og16p_s.shape, BF16)

    ii = lax.broadcasted_iota(jnp.int32, (BLOCK, BLOCK), 0)
    jj = lax.broadcasted_iota(jnp.int32, (BLOCK, BLOCK), 1)
    own_ok = jj <= ii
    first_off = jnp.where(s % tiles_per_row > 0, 0, BLOCK)
    valid_first = jnp.concatenate([jj > ii + first_off, own_ok], axis=1)
    valid_rest = jnp.concatenate([jj > ii, own_ok], axis=1)
    lo_half = lax.broadcasted_iota(jnp.int32, (2 * BLOCK, LANES), 1) < HEAD_DIM
    scale = HEAD_DIM ** -0.5

    def q_item(b):
        r0 = b * BLOCK
        u = u_s[r0:r0 + BLOCK, :]
        q1 = _dot(u, wq_ref[0][...])
        q2 = _dot(u, wq_ref[1][...])
        zhi, zlo = _split2(q1 * q1 + q2 * q2)
        sw = seg_ref.shape[0]
        ssum = jnp.concatenate([_dot(zhi[:, c:c + sw], seg_ref[...]) + _dot(zlo[:, c:c + sw], seg_ref[...])
                                for c in range(0, qw, sw)], axis=1)
        rs = lax.rsqrt(ssum * (1.0 / HEAD_DIM) + EPS)
        n1 = q1 * rs * qnw_ref[:, :qw]
        n2 = q2 * rs * qnw_ref[:, qw:]
        cos = jnp.concatenate([cos_ref[0, r0:r0 + BLOCK, :]] * (qw // LANES), axis=1)
        sin = jnp.concatenate([sin_ref[0, r0:r0 + BLOCK, :]] * (qw // LANES), axis=1)
        q16_s[r0:r0 + BLOCK, :qw] = ((n1 * cos - n2 * sin) * scale).astype(BF16)
        q16_s[r0:r0 + BLOCK, qw:] = ((n2 * cos + n1 * sin) * scale).astype(BF16)

    def gate_item(b, k):
        r0 = b * BLOCK
        gs_s[r0:r0 + BLOCK, k * WCOL:(k + 1) * WCOL] = _silu(_dot(u_s[r0:r0 + BLOCK, :], wg_ref[k][...]))

    def kv_blocks(b):
        r0 = b * BLOCK
        if b == 0:
            return ktp_ref[0], ktc_ref[0, :, 0:BLOCK], vp_ref[0], vc_ref[0, 0:BLOCK, :]
        return (ktc_ref[0, :, r0 - BLOCK:r0], ktc_ref[0, :, r0:r0 + BLOCK],
                vc_ref[0, r0 - BLOCK:r0, :], vc_ref[0, r0:r0 + BLOCK, :])

    scores = {}

    def score_item(b, g):
        r0 = b * BLOCK
        slot = (b * KV_HEADS + g) % 2
        kprev, kown, _, _ = kv_blocks(b)
        for hl in range(Q_PER_KV):
            for half in range(2):
                src = half * LANES + g * HALF
                dst = half * LANES + hl * HALF
                krhs_s[slot, dst:dst + HALF, hl * 2 * BLOCK:hl * 2 * BLOCK + BLOCK] = kprev[src:src + HALF, :]
                krhs_s[slot, dst:dst + HALF, hl * 2 * BLOCK + BLOCK:(hl + 1) * 2 * BLOCK] = kown[src:src + HALF, :]
        lhs = jnp.concatenate([q16_s[r0:r0 + BLOCK, g * LANES:(g + 1) * LANES],
                               q16_s[r0:r0 + BLOCK, qw + g * LANES:qw + (g + 1) * LANES]], axis=1)
        scores[(b, g)] = _dot(lhs, krhs_s[slot])

    def softmax_pv_item(b, g):
        r0 = b * BLOCK
        s_all = scores.pop((b, g))
        valid = valid_first if b == 0 else valid_rest
        _, _, vprev, vown = kv_blocks(b)
        vg = jnp.concatenate([vprev[:, g * LANES:(g + 1) * LANES], vown[:, g * LANES:(g + 1) * LANES]], axis=0)
        zero = jnp.zeros_like(vg)
        vbd = jnp.concatenate([jnp.where(lo_half, vg, zero), jnp.where(lo_half, zero, vg)], axis=0)
        probs = []
        for hl in range(Q_PER_KV):
            sink = sink_ref[g * Q_PER_KV + hl]
            s = jnp.where(valid, s_all[:, hl * 2 * BLOCK:(hl + 1) * 2 * BLOCK], -jnp.inf)
            m = jnp.maximum(jnp.max(s, axis=-1, keepdims=True), sink)
            e = jnp.exp(s - m)
            den = jnp.sum(e, axis=-1, keepdims=True) + jnp.exp(sink - m)
            probs.append((e * (1.0 / den)).astype(BF16))
        for pr in range(Q_PER_KV // 2):
            lhs_p = jnp.concatenate([probs[2 * pr], probs[2 * pr + 1]], axis=1)
            c0 = (g * 2 + pr) * LANES
            o_s[r0:r0 + BLOCK, c0:c0 + LANES] = _dot(lhs_p, vbd)

    def gated_item(b):
        r0 = b * BLOCK
        og16_s[r0:r0 + BLOCK, :] = (o_s[r0:r0 + BLOCK, :] * gs_s[r0:r0 + BLOCK, :]).astype(BF16)

    def out_item(b):
        r0 = b * BLOCK
        h2_s[r0:r0 + BLOCK, :] = xp_ref[0, r0:r0 + BLOCK, :] + _dotw(og16p_s[r0:r0 + BLOCK, :], ow_ref)

    def ple_item(b):
        r0 = b * BLOCK
        o_ref[0, r0:r0 + BLOCK, :] = _ple(h2_s[r0:r0 + BLOCK, :], pp_ref[0, r0:r0 + BLOCK, :],
                                          pnw_ref, pgw_ref, ppw_ref)

    @pl.when(s < n_tiles)
    def _():
        u_s[...] = _rms(x_ref[0], nw_ref[...]).astype(BF16)
        q_item(0)
        score_item(0, 0)
        for b in range(ANB):
            more = b + 1 < ANB
            for g in range(KV_HEADS):
                if g + 1 < KV_HEADS:
                    score_item(b, g + 1)
                elif more:
                    score_item(b + 1, 0)
                if g == 0:
                    gate_item(b, 0)
                    gate_item(b, 1)
                elif g == 1:
                    out_item(b)
                elif g == 2 and more:
                    q_item(b + 1)
                elif g == 3:
                    ple_item(b)
                softmax_pv_item(b, g)
            gated_item(b)
        og16p_s[...] = og16_s[...]

    @pl.when(s == n_tiles)
    def _():
        for b in range(ANB):
            out_item(b)
            ple_item(b)


def _attn_layer(h, p, layer, j, sinks, cos, sin, kt, v2, q_w16, in_w16, out_w16, gate_w16, proj_w16, nw, qnw, seg, pnw):
    bsz, seq, _ = h.shape
    n_half = ATT_WIDTH // WCOL
    wq, wq_specs = _chunks(q_w16, j, 0, n_half)
    wg, wg_specs = _chunks(in_w16, j, n_half, n_half)
    ow, ow_specs = _chunks(out_w16, j, 0, D_MODEL // WCOL)
    pgw, pgw_specs = _chunks(gate_w16, layer, 0, D_MODEL // WCOL)
    ppw, ppw_specs = _chunks(proj_w16, layer, 0, D_MODEL // WCOL)
    small = lambda *ws: [_const_spec(w.shape) for w in ws]
    consts = (nw, *wq, *wg, qnw, seg, *ow, pnw, *pgw, *ppw)
    const_specs = small(nw) + wq_specs + wg_specs + small(qnw, seg) + ow_specs + small(pnw) + pgw_specs + ppw_specs
    nt = seq // ATS
    n_tiles = bsz * nt

    def this_tile(s):
        n = jnp.minimum(s, n_tiles - 1)
        return n // nt, n % nt

    def prev_tile(s):
        n = jnp.maximum(s - 1, 0)
        return n // nt, n % nt

    def prev_blk(s):
        b, t = this_tile(s)
        return b, jnp.maximum(t * ANB - 1, 0)

    return pl.pallas_call(
        functools.partial(_attn_kernel, tiles_per_row=nt, n_tiles=n_tiles),
        grid=(n_tiles + 1,),
        in_specs=[pl.BlockSpec(memory_space=pltpu.SMEM),
                  pl.BlockSpec((1, ATS, D_MODEL), lambda s: (*this_tile(s), 0)),
                  pl.BlockSpec((1, ATS, D_MODEL), lambda s: (*prev_tile(s), 0)),
                  pl.BlockSpec((None, 1, ATS, PLE_DIM), lambda s: (layer, *prev_tile(s), 0)),
                  pl.BlockSpec((1, ATS, LANES), lambda s: (*this_tile(s), 0)),
                  pl.BlockSpec((1, ATS, LANES), lambda s: (*this_tile(s), 0)),
                  pl.BlockSpec((1, KV_WIDTH, ATS), lambda s: (this_tile(s)[0], 0, this_tile(s)[1])),
                  pl.BlockSpec((1, KV_WIDTH, BLOCK), lambda s: (prev_blk(s)[0], 0, prev_blk(s)[1])),
                  pl.BlockSpec((1, ATS, 2 * KV_WIDTH), lambda s: (*this_tile(s), 0)),
                  pl.BlockSpec((1, BLOCK, 2 * KV_WIDTH), lambda s: (*prev_blk(s), 0))]
                 + const_specs,
        out_specs=pl.BlockSpec((1, ATS, D_MODEL), lambda s: (*prev_tile(s), 0)),
        out_shape=jax.ShapeDtypeStruct(h.shape, F32),
        scratch_shapes=[pltpu.VMEM((ATS, D_MODEL), BF16),
                        pltpu.VMEM((ATS, ATT_WIDTH), BF16),
                        pltpu.VMEM((2, 2 * LANES, Q_PER_KV * 2 * BLOCK), BF16),
                        pltpu.VMEM((ATS, ATT_WIDTH), F32),
                        pltpu.VMEM((ATS, ATT_WIDTH), F32),
                        pltpu.VMEM((ATS, ATT_WIDTH), BF16),
                        pltpu.VMEM((ATS, ATT_WIDTH), BF16),
                        pltpu.VMEM((ATS, D_MODEL), F32)],
        compiler_params=pltpu.CompilerParams(dimension_semantics=("arbitrary",),
                                             vmem_limit_bytes=VMEM_LIMIT),
        name="attn_layer",
    )(sinks, h, h, p, cos, sin, kt, kt, v2, v2, *consts)


def _row(v):
    return v.reshape(1, -1).astype(F32)


def _pad_lanes(v):
    return jnp.pad(v, [(0, 0)] * (v.ndim - 1) + [(0, LANES - v.shape[-1])])


def _halves_perm(n_heads):
    d = np.arange(HALF)
    first = (np.arange(n_heads)[:, None] * HEAD_DIM + d[None, :]).reshape(-1)
    return np.concatenate([first, first + HALF])


def _segment_ones(width):
    idx = np.arange(width) // HALF
    return jnp.asarray(idx[:, None] == idx[None, :], dtype=BF16)


def kernel(x, p, positions, ssm_norm_w, ssm_in_w, ssm_conv_w, ssm_conv_b, ssm_dt_bias, ssm_a_log, ssm_d,
           ssm_gnorm_w, ssm_out_w, kv_norm_w, kv_w, k_norm_w, attn_norm_w, attn_in_w, q_norm_w, attn_sinks,
           attn_out_w, ple_norm_w, ple_gate_w, ple_proj_w):
    n_a = ssm_in_w.shape[0]
    n_b = attn_in_w.shape[0]
    head_order = np.concatenate([np.arange(0, SSM_HEADS, 2), np.arange(1, SSM_HEADS, 2)])
    ssm_in_w16 = ssm_in_w.astype(BF16)
    ssm_out_w16 = ssm_out_w.astype(BF16)
    gate_w16 = ple_gate_w.astype(BF16)
    proj_w16 = ple_proj_w.astype(BF16)
    dt_w16 = _pad_lanes(lax.slice_in_dim(ssm_in_w, D_INNER + CONV_CH, ssm_in_w.shape[2], axis=2)[:, :, head_order]
                        ).astype(BF16)
    h = x
    for i in range(n_a):
        h = _ssm_layer(
            h, p, i, ssm_in_w16, ssm_out_w16, gate_w16, proj_w16, _row(ssm_norm_w[i]), dt_w16[i],
            ssm_conv_w[i].astype(F32), _row(ssm_conv_b[i]),
            _pad_lanes(_row(ssm_dt_bias[i][head_order])), _pad_lanes(_row(ssm_a_log[i][head_order])),
            _row(jnp.repeat(ssm_d[i], SSM_HEAD_DIM)), _row(ssm_gnorm_w[i]), _row(ple_norm_w[i]))

    inv_freq = ROPE_THETA ** (-(jnp.arange(HALF, dtype=F32) * 2.0 / HEAD_DIM))
    kperm = _halves_perm(KV_HEADS)
    qperm = _halves_perm(ATT_HEADS)
    posf = positions.astype(F32)[..., None]
    wv = kv_w[:, KV_WIDTH:].reshape(D_MODEL, KV_HEADS, 1, HEAD_DIM)
    wv = jnp.broadcast_to(wv, (D_MODEL, KV_HEADS, 2, HEAD_DIM)).reshape(D_MODEL, 2 * KV_WIDTH)
    kt, v2, cos, sin = _shared_kv(
        h, posf, _row(kv_norm_w), kv_w[:, :KV_WIDTH][:, kperm].astype(BF16), wv.astype(BF16),
        _row(jnp.tile(k_norm_w.reshape(2, HALF), (1, KV_HEADS))),
        _row(jnp.tile(inv_freq, LANES // HALF)), _segment_ones(LANES))

    seg_q = _segment_ones(2 * LANES)
    attn_in_w16 = attn_in_w.astype(BF16)
    attn_q_w16 = attn_in_w[:, :, :ATT_WIDTH][:, :, qperm].astype(BF16)
    attn_out_w16 = attn_out_w.astype(BF16)
    for j in range(n_b):
        i = n_a + j
        h = _attn_layer(
            h, p, i, j, attn_sinks[j].astype(F32), cos, sin, kt, v2,
            attn_q_w16, attn_in_w16, attn_out_w16, gate_w16, proj_w16, _row(attn_norm_w[j]),
            _row(jnp.tile(q_norm_w[j].reshape(2, HALF), (1, ATT_HEADS))), seg_q, _row(ple_norm_w[i]))
    return h
```

```python
import functools

import jax
import jax.numpy as jnp
import numpy as np
from jax import lax
from jax.experimental import pallas as pl
from jax.experimental.pallas import tpu as pltpu

F32 = jnp.float32
BF16 = jnp.bfloat16

D_MODEL = 1024
D_INNER = 2048
SSM_HEAD_DIM = 64
SSM_HEADS = 32
SSM_GROUPS = 4
SSM_HPG = 8
D_STATE = 128
CONV_WIDTH = 4
BC_WIDTH = SSM_GROUPS * D_STATE
CONV_CH = D_INNER + 2 * BC_WIDTH
CHUNK = 128
ATT_HEADS = 16
KV_HEADS = 4
HEAD_DIM = 64
HALF = HEAD_DIM // 2
Q_PER_KV = 4
ATT_WIDTH = 1024
KV_WIDTH = 256
BLOCK = 128
ROPE_THETA = 10000.0
PLE_DIM = 256
EPS = 1e-6

LANES = 128
SUBLANES = 8
TS = 512
ATS = 512
ANB = ATS // BLOCK
STS = 512
SNCH = STS // CHUNK
MROWS = 256
VREGS_PER_CHUNK = CHUNK // SUBLANES
CTAIL = (CONV_WIDTH - 1) * SUBLANES
WCOL = 512
GW = D_INNER // SSM_GROUPS
PAIRS = SSM_HEADS // 2
PPG = SSM_HPG // 2
VMEM_LIMIT = 56 * 1024 * 1024

_NT = (((1,), (1,)), ((), ()))


def _dot(a, b):
    return jnp.dot(a, b, preferred_element_type=F32)


def _dotw(a, w_refs):
    return jnp.concatenate([_dot(a, w[...]) for w in w_refs], axis=1)


def _rms(x, w):
    ms = jnp.mean(x * x, axis=-1, keepdims=True)
    return x * lax.rsqrt(ms + EPS) * w


def _sigmoid(x):
    return 0.5 * jnp.tanh(0.5 * x) + 0.5


def _silu(x):
    hx = 0.5 * x
    return hx * jnp.tanh(hx) + hx


def _split2(x):
    hi = x.astype(BF16)
    lo = (x - hi.astype(F32)).astype(BF16)
    return hi, lo


def _ple_apply(h2, u2, p_tile, pgw_ref, ppw_ref):
    gate = _sigmoid(_dotw(u2, pgw_ref))
    proj = _dotw(p_tile.astype(BF16), ppw_ref)
    return h2 + gate * proj


def _ple(h2, p_tile, pnw_ref, pgw_ref, ppw_ref):
    return _ple_apply(h2, _rms(h2, pnw_ref[...]).astype(BF16), p_tile, pgw_ref, ppw_ref)


def _time_of_row(r):
    return (r & (SUBLANES - 1)) * VREGS_PER_CHUNK + (r >> 3)


def _scatter_rows(ref, tile0, row0, val, to_kernel_order):
    for tile in range(val.shape[1] // LANES):
        for i in range(val.shape[0] // SUBLANES):
            chunk, v = divmod(i, VREGS_PER_CHUNK)
            if to_kernel_order:
                rows = pl.ds(row0 + chunk * CHUNK + (CHUNK // 2) * (v % 2) + v // 2, SUBLANES, stride=SUBLANES)
            else:
                rows = pl.ds(row0 + chunk * CHUNK + v, SUBLANES, stride=VREGS_PER_CHUNK)
            piece = val[i * SUBLANES:(i + 1) * SUBLANES, tile * LANES:(tile + 1) * LANES]
            if len(ref.shape) == 3:
                ref[pl.ds(tile0 + tile, 1), rows, :] = piece[None]
            else:
                ref[rows, :] = piece


def _gather_tiles(ref, tile0, n_tiles, rows):
    return jnp.concatenate([ref[tile0 + k, rows, :] for k in range(n_tiles)], axis=1)


def _take(refs, counts):
    out, i = [], 0
    for n in counts:
        out.append(refs[i] if n is None else refs[i:i + n])
        i += 1 if n is None else n
    assert i == len(refs)
    return out


_SSM_REF_COUNTS = (None, None, None, None, D_INNER // WCOL, CONV_CH // WCOL, None, None, None, None, None, None, None,
                   D_MODEL // WCOL, None, D_MODEL // WCOL, D_MODEL // WCOL) + (None,) * 9


def _ssm_kernel(*refs, tiles_per_row, n_tiles):
    (x_ref, xp_ref, pp_ref, nw_ref, wz_ref, wxbc_ref, wdt_ref, cw_ref, cb_ref, dtb_ref, alog_ref, dsk_ref, gnw_ref,
     ow_ref, pnw_ref, pgw_ref, ppw_ref, o_ref, cbuf, u_s, dt_s, zs_s, y16_s, y16p_s, h2_s, st_s) = _take(
        refs, _SSM_REF_COUNTS)
    s = pl.program_id(0)
    row_start = s % tiles_per_row == 0

    @pl.when(s == 0)
    def _():
        y16p_s[...] = jnp.zeros(y16p_s.shape, BF16)

    @pl.when(row_start)
    def _():
        cbuf[:, 0:CTAIL, :] = jnp.zeros((CONV_CH // LANES, CTAIL, LANES), F32)
        st_s[...] = jnp.zeros(st_s.shape, F32)

    @pl.when(jnp.logical_not(row_start))
    def _():
        cbuf[:, 0:CTAIL, :] = cbuf[:, STS:STS + CTAIL, :]

    ii = lax.broadcasted_iota(jnp.int32, (CHUNK, LANES), 0)
    ll = lax.broadcasted_iota(jnp.int32, (CHUNK, LANES), 1)
    lo_half = ll < SSM_HEAD_DIM
    time_i = _time_of_row(ii)
    time_lo = _time_of_row(jnp.where(lo_half, ll, ll - SSM_HEAD_DIM))
    mask_lo = time_lo <= time_i
    mask_hi = time_lo + SUBLANES <= time_i
    ltri = (_time_of_row(ll) <= time_i).astype(F32)
    lo_row = lax.broadcasted_iota(jnp.int32, (PAIRS, LANES), 1) < SSM_HEAD_DIM
    lo_k = lax.broadcasted_iota(jnp.int32, (SSM_HEAD_DIM, LANES), 1) < SSM_HEAD_DIM

    tabs = []

    def norm_dt_item():
        u_s[...] = _rms(x_ref[0], nw_ref[...]).astype(BF16)
        _scatter_rows(dt_s, 0, 0, _dot(u_s[...], wdt_ref[...]) + dtb_ref[...], True)
        dt_raw = dt_s[...]
        dt = jnp.maximum(dt_raw, 0.0) + jnp.log1p(jnp.exp(-jnp.abs(dt_raw)))
        d_a = dt * (-jnp.exp(alog_ref[...]))
        for c in range(SNCH):
            r0 = c * CHUNK
            dt_c = dt[r0:r0 + CHUNK]
            acum = jnp.dot(ltri, d_a[r0:r0 + CHUNK], precision=lax.Precision.HIGHEST,
                           preferred_element_type=F32)
            acum_t = acum.T
            dt_t = dt_c.T
            ev, od = acum_t[:PAIRS], acum_t[PAIRS:2 * PAIRS]
            dev, dod = dt_t[:PAIRS], dt_t[PAIRS:2 * PAIRS]
            tabs.append(dict(
                acum=acum,
                w_out=jnp.exp(acum[CHUNK - 1:CHUNK, :] - acum) * dt_c,
                a_lo=jnp.where(lo_row, ev, pltpu.roll(od, SSM_HEAD_DIM, 1)),
                a_hi=jnp.where(lo_row, pltpu.roll(ev, SSM_HEAD_DIM, 1), od),
                d_lo=jnp.where(lo_row, dev, pltpu.roll(dod, SSM_HEAD_DIM, 1)),
                d_hi=jnp.where(lo_row, pltpu.roll(dev, SSM_HEAD_DIM, 1), dod)))

    def pair_expand(v, pair):
        idx = jnp.where(lo_half, pair, PAIRS + pair)
        return jnp.take_along_axis(v, idx, axis=1)

    tiles_per_wcol = WCOL // LANES

    def proj_item(hf, j):
        r0 = hf * MROWS
        _scatter_rows(cbuf, j * tiles_per_wcol, CTAIL + r0, _dot(u_s[r0:r0 + MROWS, :], wxbc_ref[j][...]), True)

    def z_item(hf, g):
        r0 = hf * MROWS
        _scatter_rows(zs_s, g * tiles_per_wcol, r0, _silu(_dot(u_s[r0:r0 + MROWS, :], wz_ref[g][...])), True)

    def conv(r0, c0, width):
        base = CTAIL + r0
        tile0, n_tiles = c0 // LANES, width // LANES
        x = _gather_tiles(cbuf, tile0, n_tiles, slice(base, base + CHUNK))
        last = lax.broadcasted_iota(jnp.int32, (SUBLANES, width), 0) == SUBLANES - 1
        wrapped = []
        for v in range(VREGS_PER_CHUNK - CONV_WIDTH + 1, VREGS_PER_CHUNK):
            prev = _gather_tiles(cbuf, tile0, n_tiles,
                                 slice(base - CHUNK + v * SUBLANES, base - CHUNK + (v + 1) * SUBLANES))
            wrapped.append(pltpu.roll(jnp.where(last, prev, x[v * SUBLANES:(v + 1) * SUBLANES]), 1, 0))
        acc = cb_ref[:, c0:c0 + width] + cw_ref[CONV_WIDTH - 1:CONV_WIDTH, c0:c0 + width] * x
        for k in range(1, CONV_WIDTH):
            shifted = jnp.concatenate(wrapped[CONV_WIDTH - 1 - k:] + [x[:CHUNK - k * SUBLANES]], axis=0)
            acc = acc + cw_ref[CONV_WIDTH - 1 - k:CONV_WIDTH - k, c0:c0 + width] * shifted
        return _silu(acc)

    prepared = {}

    def ssd_prepare(c, g):
        r0, gc, tab = c * CHUNK, g * GW, tabs[c]
        xs = conv(r0, gc, GW)
        bm = conv(r0, D_INNER + g * D_STATE, D_STATE)
        cm = conv(r0, D_INNER + BC_WIDTH + g * D_STATE, D_STATE)
        cm16, bm16 = cm.astype(BF16), bm.astype(BF16)
        h0, h1 = bm16[:SSM_HEAD_DIM], bm16[SSM_HEAD_DIM:]
        cb2 = lax.dot_general(cm16, jnp.concatenate([h0, h0, h1, h1], axis=0), _NT,
                              preferred_element_type=F32)
        cb_lo, cb_hi = cb2[:, :LANES], cb2[:, LANES:]
        a_exp = [pair_expand(tab["acum"], g * PPG + q) for q in range(PPG)]
        w_exp = jnp.concatenate([pair_expand(tab["w_out"], g * PPG + q) for q in range(PPG)], axis=1)
        a_all = jnp.concatenate(a_exp, axis=1)
        st = st_s[g]
        y_off = _dot(cm16, st.astype(BF16))
        st_s[g] = st * jnp.exp(a_all[CHUNK - 1:CHUNK, :]) + _dot(bm.T.astype(BF16), (xs * w_exp).astype(BF16))

        operands = []
        for q in range(PPG):
            pair = g * PPG + q
            t_lo = cb_lo * jnp.exp(jnp.where(mask_lo, a_exp[q] - tab["a_lo"][pair:pair + 1, :], -jnp.inf))
            t_hi = cb_hi * jnp.exp(jnp.where(mask_hi, a_exp[q] - tab["a_hi"][pair:pair + 1, :], -jnp.inf))
            lhs = jnp.concatenate([(t_lo * tab["d_lo"][pair:pair + 1, :]).astype(BF16),
                                   (t_hi * tab["d_hi"][pair:pair + 1, :]).astype(BF16)], axis=1)
            xs16 = xs[:, q * LANES:(q + 1) * LANES].astype(BF16)
            x0, x1 = xs16[:SSM_HEAD_DIM], xs16[SSM_HEAD_DIM:]
            zero = jnp.zeros_like(x0)
            rhs = jnp.concatenate([jnp.where(lo_k, x0, zero), jnp.where(lo_k, zero, x0),
                                   jnp.where(lo_k, x1, zero), jnp.where(lo_k, zero, x1)], axis=0)
            operands.append((lhs, rhs))
        prepared[(c, g)] = operands, jnp.exp(a_all) * y_off + dsk_ref[:, gc:gc + GW] * xs

    def ssd_finish(c, g):
        r0, gc = c * CHUNK, g * GW
        operands, rest = prepared.pop((c, g))
        y = jnp.concatenate([_dot(lhs, rhs) for lhs, rhs in operands], axis=1) + rest
        y = y * _gather_tiles(zs_s, gc // LANES, GW // LANES, slice(r0, r0 + CHUNK))
        y = y * lax.rsqrt(jnp.mean(y * y, axis=-1, keepdims=True) + EPS) * gnw_ref[:, gc:gc + GW]
        y16_s[r0:r0 + CHUNK, gc:gc + GW] = y.astype(BF16)

    def out_item(hf, k):
        r0 = hf * MROWS
        _scatter_rows(h2_s, k * tiles_per_wcol, r0, _dot(y16p_s[r0:r0 + MROWS, :], ow_ref[k][...]), False)

    normed = {}

    def ple_norm_item(hf):
        r0 = hf * MROWS
        h2 = xp_ref[0, r0:r0 + MROWS, :] + _gather_tiles(h2_s, 0, D_MODEL // LANES, slice(r0, r0 + MROWS))
        normed[hf] = h2, _rms(h2, pnw_ref[...]).astype(BF16)

    def ple_item(hf):
        r0 = hf * MROWS
        h2, u2 = normed.pop(hf)
        o_ref[0, r0:r0 + MROWS, :] = _ple_apply(h2, u2, pp_ref[0, r0:r0 + MROWS, :], pgw_ref, ppw_ref)

    def keep_item():
        y16p_s[...] = y16_s[...]

    n_, p_, z_, a_, f_, o_, e_, g_, k_ = (norm_dt_item, proj_item, z_item, ssd_prepare, ssd_finish, out_item,
                                          ple_norm_item, ple_item, keep_item)
    bc = (D_INNER // WCOL, D_INNER // WCOL + 1)
    order = [(n_,), (p_, 0, bc[0]), (p_, 0, bc[1]), (p_, 0, 0),
             (a_, 0, 0), (z_, 0, 0), (f_, 0, 0), (p_, 0, 1),
             (a_, 0, 1), (z_, 0, 1), (f_, 0, 1), (p_, 0, 2),
             (a_, 0, 2), (z_, 0, 2), (f_, 0, 2), (p_, 0, 3),
             (a_, 0, 3), (z_, 0, 3), (f_, 0, 3), (p_, 1, bc[0]),
             (a_, 1, 0), (p_, 1, bc[1]), (f_, 1, 0), (p_, 1, 0),
             (a_, 1, 1), (z_, 1, 0), (f_, 1, 1), (p_, 1, 1),
             (a_, 1, 2), (z_, 1, 1), (f_, 1, 2), (p_, 1, 2),
             (a_, 1, 3), (z_, 1, 2), (f_, 1, 3), (p_, 1, 3),
             (a_, 2, 0), (z_, 1, 3), (f_, 2, 0), (o_, 0, 0),
             (a_, 2, 1), (o_, 0, 1), (f_, 2, 1), (e_, 0),
             (a_, 2, 2), (f_, 2, 2), (g_, 0),
             (a_, 2, 3), (o_, 1, 0), (f_, 2, 3),
             (a_, 3, 0), (f_, 3, 0), (o_, 1, 1),
             (a_, 3, 1), (e_, 1), (f_, 3, 1),
             (a_, 3, 2), (g_, 1), (f_, 3, 2),
             (a_, 3, 3), (f_, 3, 3), (k_,)]
    flush = [(o_, 0, 0), (o_, 0, 1), (e_, 0), (o_, 1, 0), (o_, 1, 1), (g_, 0), (e_, 1), (g_, 1)]
    assert SNCH == 4 and STS == 2 * MROWS

    @pl.when(s < n_tiles)
    def _():
        for item, *args in order:
            item(*args)

    @pl.when(s == n_tiles)
    def _():
        for item, *args in flush:
            item(*args)


def _const_spec(shape):
    nd = len(shape)
    return pl.BlockSpec(shape, lambda *_: (0,) * nd, pipeline_mode=pl.Buffered(1))


def _chunk_spec(layer, col_block, rows):
    return pl.BlockSpec((None, rows, WCOL), lambda *_: (layer, 0, col_block), pipeline_mode=pl.Buffered(1))


def _chunks(w_all, layer, first_block, count):
    specs = [_chunk_spec(layer, first_block + c, w_all.shape[1]) for c in range(count)]
    return [w_all] * count, specs


def _ssm_layer(h, p, layer, in_w16, out_w16, gate_w16, proj_w16, nw, wdt, cw, cb, dtb, alog, dsk, gnw, pnw):
    bsz, seq, _ = h.shape
    wz, wz_specs = _chunks(in_w16, layer, 0, D_INNER // WCOL)
    wxbc, wxbc_specs = _chunks(in_w16, layer, D_INNER // WCOL, CONV_CH // WCOL)
    ow, ow_specs = _chunks(out_w16, layer, 0, D_MODEL // WCOL)
    pgw, pgw_specs = _chunks(gate_w16, layer, 0, D_MODEL // WCOL)
    ppw, ppw_specs = _chunks(proj_w16, layer, 0, D_MODEL // WCOL)
    small = lambda *ws: [_const_spec(w.shape) for w in ws]
    weights = (nw, *wz, *wxbc, wdt, cw, cb, dtb, alog, dsk, gnw, *ow, pnw, *pgw, *ppw)
    weight_specs = (small(nw) + wz_specs + wxbc_specs + small(wdt, cw, cb, dtb, alog, dsk, gnw) + ow_specs
                    + small(pnw) + pgw_specs + ppw_specs)
    nt = seq // STS
    n_tiles = bsz * nt

    def this_tile(s):
        n = jnp.minimum(s, n_tiles - 1)
        return n // nt, n % nt

    def prev_tile(s):
        n = jnp.maximum(s - 1, 0)
        return n // nt, n % nt

    return pl.pallas_call(
        functools.partial(_ssm_kernel, tiles_per_row=nt, n_tiles=n_tiles),
        grid=(n_tiles + 1,),
        in_specs=[pl.BlockSpec((1, STS, D_MODEL), lambda s: (*this_tile(s), 0)),
                  pl.BlockSpec((1, STS, D_MODEL), lambda s: (*prev_tile(s), 0)),
                  pl.BlockSpec((None, 1, STS, PLE_DIM), lambda s: (layer, *prev_tile(s), 0))]
                 + weight_specs,
        out_specs=pl.BlockSpec((1, STS, D_MODEL), lambda s: (*prev_tile(s), 0)),
        out_shape=jax.ShapeDtypeStruct(h.shape, F32),
        scratch_shapes=[pltpu.VMEM((CONV_CH // LANES, CTAIL + STS, LANES), F32),
                        pltpu.VMEM((STS, D_MODEL), BF16),
                        pltpu.VMEM((STS, LANES), F32),
                        pltpu.VMEM((D_INNER // LANES, STS, LANES), F32),
                        pltpu.VMEM((STS, D_INNER), BF16),
                        pltpu.VMEM((STS, D_INNER), BF16),
                        pltpu.VMEM((D_MODEL // LANES, STS, LANES), F32),
                        pltpu.VMEM((SSM_GROUPS, D_STATE, GW), F32)],
        compiler_params=pltpu.CompilerParams(dimension_semantics=("arbitrary",),
                                             vmem_limit_bytes=VMEM_LIMIT),
        name="ssm_layer",
    )(h, h, p, *weights)


def _kv_kernel(x_ref, pos_ref, nw_ref, wk_ref, wv_ref, knw_ref, invf_ref, seg_ref,
               kt_ref, v_ref, cos_ref, sin_ref):
    u = _rms(x_ref[0], nw_ref[...]).astype(BF16)
    groups = LANES // HALF
    rows = TS // groups
    lane = lax.broadcasted_iota(jnp.int32, (rows, LANES), 1)
    pos = pos_ref[0]
    packed = pos[(groups - 1) * rows:, :]
    for k in range(groups - 2, -1, -1):
        packed = jnp.where(lane < (k + 1) * HALF, pos[k * rows:(k + 1) * rows, :], packed)
    ang = packed * invf_ref[...]

    def spread(tab, k):
        y = pltpu.roll(tab, (LANES - k * HALF) % LANES, 1) if k else tab
        width = HALF
        while width < LANES:
            y = jnp.where(lane < width, y, pltpu.roll(y, width, 1))
            width *= 2
        return y

    cos4, sin4 = jnp.cos(ang), jnp.sin(ang)
    cos = jnp.concatenate([spread(cos4, k) for k in range(groups)], axis=0)
    sin = jnp.concatenate([spread(sin4, k) for k in range(groups)], axis=0)
    cos_ref[0] = cos
    sin_ref[0] = sin
    kk = _dot(u, wk_ref[...])
    k1, k2 = kk[:, :LANES], kk[:, LANES:]
    zhi, zlo = _split2(k1 * k1 + k2 * k2)
    ssum = _dot(zhi, seg_ref[...]) + _dot(zlo, seg_ref[...])
    rs = lax.rsqrt(ssum * (1.0 / HEAD_DIM) + EPS)
    n1 = k1 * rs * knw_ref[:, :LANES]
    n2 = k2 * rs * knw_ref[:, LANES:]
    kr = jnp.concatenate([n1 * cos - n2 * sin, n2 * cos + n1 * sin], axis=1)
    kt_ref[0] = kr.T.astype(BF16)
    v_ref[0] = _dot(u, wv_ref[...]).astype(BF16)


def _shared_kv(h, posf, nw, wk, wv, knw, invf, seg):
    bsz, seq, _ = h.shape
    consts = (nw, wk, wv, knw, invf, seg)
    return pl.pallas_call(
        _kv_kernel,
        grid=(bsz, seq // TS),
        in_specs=[pl.BlockSpec((1, TS, D_MODEL), lambda b, t: (b, t, 0)),
                  pl.BlockSpec((1, TS, 1), lambda b, t: (b, t, 0))]
                 + [_const_spec(w.shape) for w in consts],
        out_specs=[pl.BlockSpec((1, KV_WIDTH, TS), lambda b, t: (b, 0, t)),
                   pl.BlockSpec((1, TS, 2 * KV_WIDTH), lambda b, t: (b, t, 0)),
                   pl.BlockSpec((1, TS, LANES), lambda b, t: (b, t, 0)),
                   pl.BlockSpec((1, TS, LANES), lambda b, t: (b, t, 0))],
        out_shape=[jax.ShapeDtypeStruct((bsz, KV_WIDTH, seq), BF16),
                   jax.ShapeDtypeStruct((bsz, seq, 2 * KV_WIDTH), BF16),
                   jax.ShapeDtypeStruct((bsz, seq, LANES), F32),
                   jax.ShapeDtypeStruct((bsz, seq, LANES), F32)],
        compiler_params=pltpu.CompilerParams(dimension_semantics=("arbitrary", "arbitrary"),
                                             vmem_limit_bytes=VMEM_LIMIT),
        name="shared_kv",
    )(h, posf, *consts)


_ATTN_REF_COUNTS = (None,) * 11 + (ATT_WIDTH // WCOL, ATT_WIDTH // WCOL, None, None, D_MODEL // WCOL, None,
                                   D_MODEL // WCOL, D_MODEL // WCOL) + (None,) * 9


def _attn_kernel(*refs, tiles_per_row, n_tiles):
    (sink_ref, x_ref, xp_ref, pp_ref, cos_ref, sin_ref, ktc_ref, ktp_ref, vc_ref, vp_ref, nw_ref, wq_ref, wg_ref,
     qnw_ref, seg_ref, ow_ref, pnw_ref, pgw_ref, ppw_ref, o_ref, u_s, q16_s, krhs_s, gs_s, o_s, og16_s, og16p_s,
     h2_s) = _take(refs, _ATTN_REF_COUNTS)
    s = pl.program_id(0)
    qw = ATT_WIDTH // 2

    @pl.when(s == 0)
    def _():
        krhs_s[...] = jnp.zeros(krhs_s.shape, BF16)
        og16p_s[...] = jnp.zeros(og16p_s.shape, BF16)

    ii = lax.broadcasted_iota(jnp.int32, (BLOCK, BLOCK), 0)
    jj = lax.broadcasted_iota(jnp.int32, (BLOCK, BLOCK), 1)
    own_ok = jj <= ii
    first_off = jnp.where(s % tiles_per_row > 0, 0, BLOCK)
    valid_first = jnp.concatenate([jj > ii + first_off, own_ok], axis=1)
    valid_rest = jnp.concatenate([jj > ii, own_ok], axis=1)
    lo_half = lax.broadcasted_iota(jnp.int32, (2 * BLOCK, LANES), 1) < HEAD_DIM
    scale = HEAD_DIM ** -0.5

    def q_item(b):
        r0 = b * BLOCK
        u = u_s[r0:r0 + BLOCK, :]
        q1 = _dot(u, wq_ref[0][...])
        q2 = _dot(u, wq_ref[1][...])
        zhi, zlo = _split2(q1 * q1 + q2 * q2)
        sw = seg_ref.shape[0]
        ssum = jnp.concatenate([_dot(zhi[:, c:c + sw], seg_ref[...]) + _dot(zlo[:, c:c + sw], seg_ref[...])
                                for c in range(0, qw, sw)], axis=1)
        rs = lax.rsqrt(ssum * (1.0 / HEAD_DIM) + EPS)
        n1 = q1 * rs * qnw_ref[:, :qw]
        n2 = q2 * rs * qnw_ref[:, qw:]
        cos = jnp.concatenate([cos_ref[0, r0:r0 + BLOCK, :]] * (qw // LANES), axis=1)
        sin = jnp.concatenate([sin_ref[0, r0:r0 + BLOCK, :]] * (qw // LANES), axis=1)
        q16_s[r0:r0 + BLOCK, :qw] = ((n1 * cos - n2 * sin) * scale).astype(BF16)
        q16_s[r0:r0 + BLOCK, qw:] = ((n2 * cos + n1 * sin) * scale).astype(BF16)

    def gate_item(b, k):
        r0 = b * BLOCK
        gs_s[r0:r0 + BLOCK, k * WCOL:(k + 1) * WCOL] = _silu(_dot(u_s[r0:r0 + BLOCK, :], wg_ref[k][...]))

    def kv_blocks(b):
        r0 = b * BLOCK
        if b == 0:
            return ktp_ref[0], ktc_ref[0, :, 0:BLOCK], vp_ref[0], vc_ref[0, 0:BLOCK, :]
        return (ktc_ref[0, :, r0 - BLOCK:r0], ktc_ref[0, :, r0:r0 + BLOCK],
                vc_ref[0, r0 - BLOCK:r0, :], vc_ref[0, r0:r0 + BLOCK, :])

    scores = {}

    def score_item(b, g):
        r0 = b * BLOCK
        slot = (b * KV_HEADS + g) % 2
        kprev, kown, _, _ = kv_blocks(b)
        for hl in range(Q_PER_KV):
            for half in range(2):
                src = half * LANES + g * HALF
                dst = half * LANES + hl * HALF
                krhs_s[slot, dst:dst + HALF, hl * 2 * BLOCK:hl * 2 * BLOCK + BLOCK] = kprev[src:src + HALF, :]
                krhs_s[slot, dst:dst + HALF, hl * 2 * BLOCK + BLOCK:(hl + 1) * 2 * BLOCK] = kown[src:src + HALF, :]
        lhs = jnp.concatenate([q16_s[r0:r0 + BLOCK, g * LANES:(g + 1) * LANES],
                               q16_s[r0:r0 + BLOCK, qw + g * LANES:qw + (g + 1) * LANES]], axis=1)
        scores[(b, g)] = _dot(lhs, krhs_s[slot])

    def softmax_pv_item(b, g):
        r0 = b * BLOCK
        s_all = scores.pop((b, g))
        valid = valid_first if b == 0 else valid_rest
        _, _, vprev, vown = kv_blocks(b)
        vg = jnp.concatenate([vprev[:, g * LANES:(g + 1) * LANES], vown[:, g * LANES:(g + 1) * LANES]], axis=0)
        zero = jnp.zeros_like(vg)
        vbd = jnp.concatenate([jnp.where(lo_half, vg, zero), jnp.where(lo_half, zero, vg)], axis=0)
        probs = []
        for hl in range(Q_PER_KV):
            sink = sink_ref[g * Q_PER_KV + hl]
            s = jnp.where(valid, s_all[:, hl * 2 * BLOCK:(hl + 1) * 2 * BLOCK], -jnp.inf)
            m = jnp.maximum(jnp.max(s, axis=-1, keepdims=True), sink)
            e = jnp.exp(s - m)
            den = jnp.sum(e, axis=-1, keepdims=True) + jnp.exp(sink - m)
            probs.append((e * (1.0 / den)).astype(BF16))
        for pr in range(Q_PER_KV // 2):
            lhs_p = jnp.concatenate([probs[2 * pr], probs[2 * pr + 1]], axis=1)
            c0 = (g * 2 + pr) * LANES
            o_s[r0:r0 + BLOCK, c0:c0 + LANES] = _dot(lhs_p, vbd)

    def gated_item(b):
        r0 = b * BLOCK
        og16_s[r0:r0 + BLOCK, :] = (o_s[r0:r0 + BLOCK, :] * gs_s[r0:r0 + BLOCK, :]).astype(BF16)

    def out_item(b):
        r0 = b * BLOCK
        h2_s[r0:r0 + BLOCK, :] = xp_ref[0, r0:r0 + BLOCK, :] + _dotw(og16p_s[r0:r0 + BLOCK, :], ow_ref)

    def ple_item(b):
        r0 = b * BLOCK
        o_ref[0, r0:r0 + BLOCK, :] = _ple(h2_s[r0:r0 + BLOCK, :], pp_ref[0, r0:r0 + BLOCK, :],
                                          pnw_ref, pgw_ref, ppw_ref)

    @pl.when(s < n_tiles)
    def _():
        u_s[...] = _rms(x_ref[0], nw_ref[...]).astype(BF16)
        q_item(0)
        score_item(0, 0)
        for b in range(ANB):
            more = b + 1 < ANB
            for g in range(KV_HEADS):
                if g + 1 < KV_HEADS:
                    score_item(b, g + 1)
                elif more:
                    score_item(b + 1, 0)
                if g == 0:
                    gate_item(b, 0)
                    gate_item(b, 1)
                elif g == 1:
                    out_item(b)
                elif g == 2 and more:
                    q_item(b + 1)
                elif g == 3:
                    ple_item(b)
                softmax_pv_item(b, g)
            gated_item(b)
        og16p_s[...] = og16_s[...]

    @pl.when(s == n_tiles)
    def _():
        for b in range(ANB):
            out_item(b)
            ple_item(b)


def _attn_layer(h, p, layer, j, sinks, cos, sin, kt, v2, q_w16, in_w16, out_w16, gate_w16, proj_w16, nw, qnw, seg, pnw):
    bsz, seq, _ = h.shape
    n_half = ATT_WIDTH // WCOL
    wq, wq_specs = _chunks(q_w16, j, 0, n_half)
    wg, wg_specs = _chunks(in_w16, j, n_half, n_half)
    ow, ow_specs = _chunks(out_w16, j, 0, D_MODEL // WCOL)
    pgw, pgw_specs = _chunks(gate_w16, layer, 0, D_MODEL // WCOL)
    ppw, ppw_specs = _chunks(proj_w16, layer, 0, D_MODEL // WCOL)
    small = lambda *ws: [_const_spec(w.shape) for w in ws]
    consts = (nw, *wq, *wg, qnw, seg, *ow, pnw, *pgw, *ppw)
    const_specs = small(nw) + wq_specs + wg_specs + small(qnw, seg) + ow_specs + small(pnw) + pgw_specs + ppw_specs
    nt = seq // ATS
    n_tiles = bsz * nt

    def this_tile(s):
        n = jnp.minimum(s, n_tiles - 1)
        return n // nt, n % nt

    def prev_tile(s):
        n = jnp.maximum(s - 1, 0)
        return n // nt, n % nt

    def prev_blk(s):
        b, t = this_tile(s)
        return b, jnp.maximum(t * ANB - 1, 0)

    return pl.pallas_call(
        functools.partial(_attn_kernel, tiles_per_row=nt, n_tiles=n_tiles),
        grid=(n_tiles + 1,),
        in_specs=[pl.BlockSpec(memory_space=pltpu.SMEM),
                  pl.BlockSpec((1, ATS, D_MODEL), lambda s: (*this_tile(s), 0)),
                  pl.BlockSpec((1, ATS, D_MODEL), lambda s: (*prev_tile(s), 0)),
                  pl.BlockSpec((None, 1, ATS, PLE_DIM), lambda s: (layer, *prev_tile(s), 0)),
                  pl.BlockSpec((1, ATS, LANES), lambda s: (*this_tile(s), 0)),
                  pl.BlockSpec((1, ATS, LANES), lambda s: (*this_tile(s), 0)),
                  pl.BlockSpec((1, KV_WIDTH, ATS), lambda s: (this_tile(s)[0], 0, this_tile(s)[1])),
                  pl.BlockSpec((1, KV_WIDTH, BLOCK), lambda s: (prev_blk(s)[0], 0, prev_blk(s)[1])),
                  pl.BlockSpec((1, ATS, 2 * KV_WIDTH), lambda s: (*this_tile(s), 0)),
                  pl.BlockSpec((1, BLOCK, 2 * KV_WIDTH), lambda s: (*prev_blk(s), 0))]
                 + const_specs,
        out_specs=pl.BlockSpec((1, ATS, D_MODEL), lambda s: (*prev_tile(s), 0)),
        out_shape=jax.ShapeDtypeStruct(h.shape, F32),
        scratch_shapes=[pltpu.VMEM((ATS, D_MODEL), BF16),
                        pltpu.VMEM((ATS, ATT_WIDTH), BF16),
                        pltpu.VMEM((2, 2 * LANES, Q_PER_KV * 2 * BLOCK), BF16),
                        pltpu.VMEM((ATS, ATT_WIDTH), F32),
                        pltpu.VMEM((ATS, ATT_WIDTH), F32),
                        pltpu.VMEM((ATS, ATT_WIDTH), BF16),
                        pltpu.VMEM((ATS, ATT_WIDTH), BF16),
                        pltpu.VMEM((ATS, D_MODEL), F32)],
        compiler_params=pltpu.CompilerParams(dimension_semantics=("arbitrary",),
                                             vmem_limit_bytes=VMEM_LIMIT),
        name="attn_layer",
    )(sinks, h, h, p, cos, sin, kt, kt, v2, v2, *consts)


def _row(v):
    return v.reshape(1, -1).astype(F32)


def _pad_lanes(v):
    return jnp.pad(v, [(0, 0)] * (v.ndim - 1) + [(0, LANES - v.shape[-1])])


def _halves_perm(n_heads):
    d = np.arange(HALF)
    first = (np.arange(n_heads)[:, None] * HEAD_DIM + d[None, :]).reshape(-1)
    return np.concatenate([first, first + HALF])


def _segment_ones(width):
    idx = np.arange(width) // HALF
    return jnp.asarray(idx[:, None] == idx[None, :], dtype=BF16)


def kernel(x, p, positions, ssm_norm_w, ssm_in_w, ssm_conv_w, ssm_conv_b, ssm_dt_bias, ssm_a_log, ssm_d,
           ssm_gnorm_w, ssm_out_w, kv_norm_w, kv_w, k_norm_w, attn_norm_w, attn_in_w, q_norm_w, attn_sinks,
           attn_out_w, ple_norm_w, ple_gate_w, ple_proj_w):
    n_a = ssm_in_w.shape[0]
    n_b = attn_in_w.shape[0]
    head_order = np.concatenate([np.arange(0, SSM_HEADS, 2), np.arange(1, SSM_HEADS, 2)])
    ssm_in_w16 = ssm_in_w.astype(BF16)
    ssm_out_w16 = ssm_out_w.astype(BF16)
    gate_w16 = ple_gate_w.astype(BF16)
    proj_w16 = ple_proj_w.astype(BF16)
    dt_w16 = _pad_lanes(lax.slice_in_dim(ssm_in_w, D_INNER + CONV_CH, ssm_in_w.shape[2], axis=2)[:, :, head_order]
                        ).astype(BF16)
    h = x
    for i in range(n_a):
        h = _ssm_layer(
            h, p, i, ssm_in_w16, ssm_out_w16, gate_w16, proj_w16, _row(ssm_norm_w[i]), dt_w16[i],
            ssm_conv_w[i].astype(F32), _row(ssm_conv_b[i]),
            _pad_lanes(_row(ssm_dt_bias[i][head_order])), _pad_lanes(_row(ssm_a_log[i][head_order])),
            _row(jnp.repeat(ssm_d[i], SSM_HEAD_DIM)), _row(ssm_gnorm_w[i]), _row(ple_norm_w[i]))

    inv_freq = ROPE_THETA ** (-(jnp.arange(HALF, dtype=F32) * 2.0 / HEAD_DIM))
    kperm = _halves_perm(KV_HEADS)
    qperm = _halves_perm(ATT_HEADS)
    posf = positions.astype(F32)[..., None]
    wv = kv_w[:, KV_WIDTH:].reshape(D_MODEL, KV_HEADS, 1, HEAD_DIM)
    wv = jnp.broadcast_to(wv, (D_MODEL, KV_HEADS, 2, HEAD_DIM)).reshape(D_MODEL, 2 * KV_WIDTH)
    kt, v2, cos, sin = _shared_kv(
        h, posf, _row(kv_norm_w), kv_w[:, :KV_WIDTH][:, kperm].astype(BF16), wv.astype(BF16),
        _row(jnp.tile(k_norm_w.reshape(2, HALF), (1, KV_HEADS))),
        _row(jnp.tile(inv_freq, LANES // HALF)), _segment_ones(LANES))

    seg_q = _segment_ones(2 * LANES)
    attn_in_w16 = attn_in_w.astype(BF16)
    attn_q_w16 = attn_in_w[:, :, :ATT_WIDTH][:, :, qperm].astype(BF16)
    attn_out_w16 = attn_out_w.astype(BF16)
    for j in range(n_b):
        i = n_a + j
        h = _attn_layer(
            h, p, i, j, attn_sinks[j].astype(F32), cos, sin, kt, v2,
            attn_q_w16, attn_in_w16, attn_out_w16, gate_w16, proj_w16, _row(attn_norm_w[j]),
            _row(jnp.tile(q_norm_w[j].reshape(2, HALF), (1, ATT_HEADS))), seg_q, _row(ple_norm_w[i]))
    return h
```
